```python
import jax, jax.numpy as jnp
from jax import lax
import numpy as np

D_MODEL = 1024
BATCH = 8
SEQ = 4096
DEPTH = 4

GRID_W = 64
CTX_LEN = 256
N_MIXERS = 3
QUERY_BLOCK = 128
NORM_EPS = 1e-6
ROPE_BASE = 10000.0

A_HEADS = 16
A_KV_HEADS = 4
A_GROUP = A_HEADS // A_KV_HEADS
A_HEAD_DIM = D_MODEL // A_HEADS
F_GROUPS = 4
M_HEADS = 16
M_Q_LORA = 3 * D_MODEL // 8
M_KV_LORA = D_MODEL // 4
M_NOPE = D_MODEL // 16
M_ROPE = D_MODEL // 32
M_V = D_MODEL // 16
FF_DENSE = 11 * D_MODEL // 4
N_EXPERTS = 8
TOP_K = 2
FF_EXPERT = 7 * D_MODEL // 2

N_A = (DEPTH + 2) // 3
N_B = (DEPTH + 1) // 3
N_C = DEPTH // 3
N_DENSE = (DEPTH + 1) // 2
N_MOE = DEPTH // 2

kernel_name = "hybrid_gqa_fnet_mla_moe_dit"


def rmsnorm(x, g):
    x32 = x.astype(jnp.float32)
    y = x32 * lax.rsqrt(jnp.mean(x32 * x32, axis=-1, keepdims=True) + NORM_EPS)
    return (y * g.astype(jnp.float32)).astype(x.dtype)


def modulate(h, shift, scale):
    return h * (1 + scale) + shift


def axial_rope_tables(rows, cols, rot_dim):
    nf = rot_dim // 4
    inv = ROPE_BASE ** (-jnp.arange(nf, dtype=jnp.float32) / nf)
    ang_r = rows[:, None] * inv[None, :]
    ang_c = cols[:, None] * inv[None, :]
    return (jnp.cos(ang_r), jnp.sin(ang_r), jnp.cos(ang_c), jnp.sin(ang_c))


def _rotate(x, cos, sin):
    x1, x2 = jnp.split(x, 2, axis=-1)
    cos = cos[None, :, None, :]
    sin = sin[None, :, None, :]
    return jnp.concatenate([x1 * cos - x2 * sin, x2 * cos + x1 * sin], axis=-1)


def apply_axial_rope(x, tabs):
    cr, sr, cc, sc = tabs
    xr, xc = jnp.split(x, 2, axis=-1)
    out = jnp.concatenate([_rotate(xr, cr, sr), _rotate(xc, cc, sc)], axis=-1)
    return out.astype(x.dtype)


def sweep_attention(q, k, v, scale):
    B, Sq, KV, G, dk = q.shape
    dv = v.shape[-1]
    nb = Sq // QUERY_BLOCK
    qb = q.reshape(B, nb, QUERY_BLOCK, KV, G, dk).transpose(1, 0, 2, 3, 4, 5)

    def one_block(qblk):
        s = jnp.einsum('bqkgd,bskd->bkgqs', qblk, k, preferred_element_type=jnp.float32) * scale
        p = jax.nn.softmax(s, axis=-1).astype(v.dtype)
        return jnp.einsum('bkgqs,bskd->bqkgd', p, v)

    o = lax.map(one_block, qb)
    return o.transpose(1, 0, 2, 3, 4, 5).reshape(B, Sq, KV * G * dv)


def _gqa_project(h, w_qkv, g_q, g_k, with_q):
    B, S, _ = h.shape
    nq = A_HEADS * A_HEAD_DIM
    nkv = A_KV_HEADS * A_HEAD_DIM
    out = h @ (w_qkv if with_q else w_qkv[:, nq:])
    q = None
    if with_q:
        q = rmsnorm(out[..., :nq].reshape(B, S, A_HEADS, A_HEAD_DIM), g_q)
        out = out[..., nq:]
    k = rmsnorm(out[..., :nkv].reshape(B, S, A_KV_HEADS, A_HEAD_DIM), g_k)
    v = out[..., nkv:].reshape(B, S, A_KV_HEADS, A_HEAD_DIM)
    return q, k, v


def gqa_mixer(hc, hl, w_qkv, g_q, g_k, w_o, rope, ctx_queries):
    scale = A_HEAD_DIM ** -0.5
    ql, kl, vl = _gqa_project(hl, w_qkv, g_q, g_k, True)
    ql = apply_axial_rope(ql, rope)
    kl = apply_axial_rope(kl, rope)
    qc, kc, vc = _gqa_project(hc, w_qkv, g_q, g_k, ctx_queries)
    k_all = jnp.concatenate([kc, kl], axis=1)
    v_all = jnp.concatenate([vc, vl], axis=1)
    B, S = hl.shape[:2]
    yl = sweep_attention(ql.reshape(B, S, A_KV_HEADS, A_GROUP, A_HEAD_DIM), k_all, v_all, scale) @ w_o
    yc = None
    if ctx_queries:
        L = hc.shape[1]
        yc = sweep_attention(qc.reshape(B, L, A_KV_HEADS, A_GROUP, A_HEAD_DIM), kc, vc, scale) @ w_o
    return yc, yl


def fourier_mixer(h, w_f):
    B, S, D = h.shape
    hg = h.astype(jnp.float32).reshape(B, S, F_GROUPS, D // F_GROUPS)
    f = jnp.fft.fftn(hg, axes=(1, 3), norm='ortho').real
    return f.reshape(B, S, D).astype(h.dtype) @ w_f


def _mla_project(h, w_down, g_cq, g_ckv, w_uq, w_ukv, rope, with_q):
    B, S, _ = h.shape
    d = h @ (w_down if with_q else w_down[:, M_Q_LORA:])
    cq = None
    if with_q:
        cq = d[..., :M_Q_LORA]
        d = d[..., M_Q_LORA:]
    ckv = d[..., :M_KV_LORA]
    kr = d[..., M_KV_LORA:].reshape(B, S, 1, M_ROPE)
    kv = (rmsnorm(ckv, g_ckv) @ w_ukv).reshape(B, S, M_HEADS, M_NOPE + M_V)
    k_nope, v = kv[..., :M_NOPE], kv[..., M_NOPE:]
    if rope is not None:
        kr = apply_axial_rope(kr, rope)
    k = jnp.concatenate([k_nope, jnp.broadcast_to(kr, (B, S, M_HEADS, M_ROPE))], axis=-1)
    q = None
    if with_q:
        q = (rmsnorm(cq, g_cq) @ w_uq).reshape(B, S, M_HEADS, M_NOPE + M_ROPE)
        if rope is not None:
            q = jnp.concatenate([q[..., :M_NOPE], apply_axial_rope(q[..., M_NOPE:], rope)], axis=-1)
    return q, k, v


def mla_mixer(hc, hl, w_down, g_cq, g_ckv, w_uq, w_ukv, w_o, rope, ctx_queries):
    scale = (M_NOPE + M_ROPE) ** -0.5
    ql, kl, vl = _mla_project(hl, w_down, g_cq, g_ckv, w_uq, w_ukv, rope, True)
    qc, kc, vc = _mla_project(hc, w_down, g_cq, g_ckv, w_uq, w_ukv, None, ctx_queries)
    k_all = jnp.concatenate([kc, kl], axis=1)
    v_all = jnp.concatenate([vc, vl], axis=1)
    yl = sweep_attention(ql[:, :, :, None, :], k_all, v_all, scale) @ w_o
    yc = None
    if ctx_queries:
        yc = sweep_attention(qc[:, :, :, None, :], kc, vc, scale) @ w_o
    return yc, yl


def swiglu(h, w_gu, w_down):
    g, u = jnp.split(h @ w_gu, 2, axis=-1)
    return (jax.nn.silu(g) * u) @ w_down


def moe_swiglu(h, w_router, w_gu, w_down):
    B, S, D = h.shape
    t = h.reshape(B * S, D)
    logits = (t @ w_router).astype(jnp.float32)
    top_v, top_i = lax.top_k(logits, TOP_K)
    wts = jax.nn.softmax(top_v, axis=-1)
    gates = jnp.sum(jax.nn.one_hot(top_i, N_EXPERTS, dtype=jnp.float32) * wts[..., None], axis=1).astype(h.dtype)
    out = jnp.zeros_like(t)
    for e in range(N_EXPERTS):
        out = out + gates[:, e:e + 1] * swiglu(t, w_gu[e], w_down[e])
    return out.reshape(B, S, D)


def setup_inputs(seed: int = 0) -> dict:
    key = jax.random.key(seed)
    ks = jax.random.split(key, 24)
    D = D_MODEL

    def nrm(k, shape, std):
        return jax.random.normal(k, shape, dtype=jnp.float32) * std

    def gain(k, shape):
        return 1.0 + 0.02 * jax.random.normal(k, shape, dtype=jnp.float32)

    return {
        'x': nrm(ks[0], (BATCH, SEQ, D), 1.0),
        'c': nrm(ks[1], (BATCH, D), 1.0),
        'ctx': nrm(ks[2], (BATCH, CTX_LEN, D), 1.0),
        'c_ctx': nrm(ks[3], (D,), 1.0),
        'w_mod': nrm(ks[4], (DEPTH, D, 6 * D), 0.5 * D ** -0.5),
        'b_mod': nrm(ks[5], (DEPTH, 6 * D), 0.02),
        'norm_g': gain(ks[6], (DEPTH, 4, D)),
        'a_w_qkv': nrm(ks[7], (N_A, D, (A_HEADS + 2 * A_KV_HEADS) * A_HEAD_DIM), D ** -0.5),
        'a_g_q': gain(ks[8], (N_A, A_HEAD_DIM)),
        'a_g_k': gain(ks[9], (N_A, A_HEAD_DIM)),
        'a_w_o': nrm(ks[10], (N_A, A_HEADS * A_HEAD_DIM, D), (A_HEADS * A_HEAD_DIM) ** -0.5),
        'f_w': nrm(ks[11], (N_B, D, D), D ** -0.5),
        'm_w_down': nrm(ks[12], (N_C, D, M_Q_LORA + M_KV_LORA + M_ROPE), D ** -0.5),
        'm_g_cq': gain(ks[13], (N_C, M_Q_LORA)),
        'm_g_ckv': gain(ks[14], (N_C, M_KV_LORA)),
        'm_w_uq': nrm(ks[15], (N_C, M_Q_LORA, M_HEADS * (M_NOPE + M_ROPE)), M_Q_LORA ** -0.5),
        'm_w_ukv': nrm(ks[16], (N_C, M_KV_LORA, M_HEADS * (M_NOPE + M_V)), M_KV_LORA ** -0.5),
        'm_w_o': nrm(ks[17], (N_C, M_HEADS * M_V, D), (M_HEADS * M_V) ** -0.5),
        'd_w_gu': nrm(ks[18], (N_DENSE, D, 2 * FF_DENSE), D ** -0.5),
        'd_w_down': nrm(ks[19], (N_DENSE, FF_DENSE, D), FF_DENSE ** -0.5),
        'e_w_router': nrm(ks[20], (N_MOE, D, N_EXPERTS), D ** -0.5),
        'e_w_gu': nrm(ks[21], (N_MOE, N_EXPERTS, D, 2 * FF_EXPERT), D ** -0.5),
        'e_w_down': nrm(ks[22], (N_MOE, N_EXPERTS, FF_EXPERT, D), FF_EXPERT ** -0.5),
    }


def reference(x, c, ctx, c_ctx, w_mod, b_mod, norm_g, a_w_qkv, a_g_q, a_g_k, a_w_o, f_w,
              m_w_down, m_g_cq, m_g_ckv, m_w_uq, m_w_ukv, m_w_o, d_w_gu, d_w_down,
              e_w_router, e_w_gu, e_w_down):
    B, S, D = x.shape
    L = ctx.shape[1]
    ROWS = S // GRID_W
    rows = jnp.repeat(jnp.arange(ROWS, dtype=jnp.float32), GRID_W)
    cols = jnp.tile(jnp.arange(GRID_W, dtype=jnp.float32), ROWS)
    rope_a = axial_rope_tables(rows, cols, A_HEAD_DIM)
    rope_m = axial_rope_tables(rows, cols, M_ROPE)
    sc_lat = jax.nn.silu(c)
    sc_ctx = jax.nn.silu(c_ctx)
    xl, xc = x, ctx

    for i in range(DEPTH):
        last = i == DEPTH - 1
        kind = i % N_MIXERS
        j = i // N_MIXERS
        ctx_in = (not last) or kind != 1

        mod_l = sc_lat @ w_mod[i] + b_mod[i]
        sh_m, s_m, g_m, sh_f, s_f, g_f = [t[:, None, :] for t in jnp.split(mod_l, 6, axis=-1)]
        hl = modulate(rmsnorm(xl, norm_g[i, 0]), sh_m, s_m)
        hc = None
        mc = None
        if ctx_in:
            n_c = 6 * D if not last else 2 * D
            mod_c = sc_ctx @ w_mod[i, :, :n_c] + b_mod[i, :n_c]
            mc = jnp.split(mod_c, n_c // D)
            hc = modulate(rmsnorm(xc, norm_g[i, 0]), mc[0], mc[1])

        if kind == 0:
            yc, yl = gqa_mixer(hc, hl, a_w_qkv[j], a_g_q[j], a_g_k[j], a_w_o[j], rope_a, not last)
        elif kind == 1:
            yl = fourier_mixer(hl, f_w[j])
            yc = fourier_mixer(hc, f_w[j]) if not last else None
        else:
            yc, yl = mla_mixer(hc, hl, m_w_down[j], m_g_cq[j], m_g_ckv[j], m_w_uq[j], m_w_ukv[j],
                               m_w_o[j], rope_m, not last)
        xl = xl + g_m * rmsnorm(yl, norm_g[i, 1])
        if not last:
            xc = xc + mc[2] * rmsnorm(yc, norm_g[i, 1])

        hl2 = modulate(rmsnorm(xl, norm_g[i, 2]), sh_f, s_f)
        if not last:
            hc2 = modulate(rmsnorm(xc, norm_g[i, 2]), mc[3], mc[4])
            h2 = jnp.concatenate([hc2, hl2], axis=1)
        else:
            h2 = hl2
        if i % 2 == 0:
            y2 = swiglu(h2, d_w_gu[i // 2], d_w_down[i // 2])
        else:
            y2 = moe_swiglu(h2, e_w_router[i // 2], e_w_gu[i // 2], e_w_down[i // 2])
        if not last:
            xc = xc + mc[5] * rmsnorm(y2[:, :L], norm_g[i, 3])
            yl2 = y2[:, L:]
        else:
            yl2 = y2
        xl = xl + g_f * rmsnorm(yl2, norm_g[i, 3])

    return xl
```

```python
import functools
import math

import jax
import jax.numpy as jnp
from jax import lax
from jax.experimental import pallas as pl
from jax.experimental.pallas import tpu as pltpu

F32 = jnp.float32
BF16 = jnp.bfloat16

D_MODEL = 1024
GRID_W = 64
NORM_EPS = 1e-6
ROPE_BASE = 10000.0
LANES = 128
HALF = LANES // 2

A_HEADS = 16
A_KV_HEADS = 4
A_HEAD_DIM = 64
F_GROUPS = 4
M_HEADS = 16
M_Q_LORA = 384
M_KV_LORA = 256
M_NOPE = 64
M_ROPE = 32
M_V = 64
N_EXPERTS = 8
N_MIXERS = 3

VMEM_LIMIT = 56 * 1024 * 1024
N_MOD_ROWS = 16


def _cparams(n_axes):
    return pltpu.CompilerParams(
        dimension_semantics=("arbitrary",) * n_axes, vmem_limit_bytes=VMEM_LIMIT)


def _rms(x, g):
    y = x * lax.rsqrt(jnp.mean(x * x, axis=-1, keepdims=True) + NORM_EPS)
    return y * g


def _norm_mod(x, g, shift, scale):
    return _rms(x, g) * (1.0 + scale) + shift


def _half_rms(y, g):
    lo = lax.broadcasted_iota(jnp.int32, y.shape, 1) < HALF
    s = y * y
    s_lo = jnp.sum(jnp.where(lo, s, 0.0), axis=-1, keepdims=True)
    s_hi = jnp.sum(jnp.where(lo, 0.0, s), axis=-1, keepdims=True)
    ms = jnp.where(lo, s_lo, s_hi) * (1.0 / HALF)
    return (y * lax.rsqrt(ms + NORM_EPS)) * g


def _rope_chunk(y, cos, sin, width):
    lane = lax.broadcasted_iota(jnp.int32, y.shape, 1)
    first = ((lane // width) % 2) == 0
    partner = jnp.where(first, pltpu.roll(y, LANES - width, 1), pltpu.roll(y, width, 1))
    return y * cos + partner * sin


def _dup_halves(y):
    lo = lax.broadcasted_iota(jnp.int32, y.shape, 1) < HALF
    r = pltpu.roll(y, HALF, 1)
    return jnp.where(lo, y, r), jnp.where(lo, r, y)


def _dot(a, b):
    return jnp.dot(a, b, preferred_element_type=F32)


def _mod_kernel(c_ref, w_ref, b_ref, o_ref):
    c = c_ref[...]
    sc = c * jax.nn.sigmoid(c)
    o_ref[...] = jnp.dot(sc, w_ref[...], precision=lax.Precision.HIGHEST,
                         preferred_element_type=F32) + b_ref[...]


def _mod_vectors(cvec, w_mod, b_mod):
    depth, d, n = w_mod.shape
    tn = 1536
    out = pl.pallas_call(
        _mod_kernel,
        grid=(depth, n // tn),
        in_specs=[
            pl.BlockSpec((N_MOD_ROWS, d), lambda i, j: (0, 0)),
            pl.BlockSpec((None, d, tn), lambda i, j: (i, 0, j)),
            pl.BlockSpec((None, 1, tn), lambda i, j: (i, 0, j)),
        ],
        out_specs=pl.BlockSpec((None, N_MOD_ROWS, tn), lambda i, j: (i, 0, j)),
        out_shape=jax.ShapeDtypeStruct((depth, N_MOD_ROWS, n), F32),
        compiler_params=_cparams(2),
        name="mod_vectors",
    )(cvec, w_mod, b_mod.reshape(depth, 1, n))
    return out.reshape(depth, N_MOD_ROWS, 6, d)


class _Geom:
    def __init__(self, B, S, L, tm):
        assert S % tm == 0 and (B * L) % tm == 0
        self.B, self.S, self.L, self.tm = B, S, L, tm
        self.n_lat = B * S // tm
        self.n_all = (B * S + B * L) // tm
        self.per_batch = S // tm

    def mod_map(self, i):
        return (jnp.where(i < self.n_lat, i // self.per_batch, self.B), 0, 0)

    def table_map(self, i):
        return (jnp.where(i < self.n_lat, i % self.per_batch, self.per_batch), 0)

    def mod_spec(self):
        return pl.BlockSpec((None, 6, D_MODEL), lambda i: self.mod_map(i))


def _row_spec(tm, n):
    return pl.BlockSpec((tm, n), lambda i: (i, 0))


def _full_spec(shape):
    nd = len(shape)
    return pl.BlockSpec(shape, lambda i: (0,) * nd)


def _gqa_proj_kernel(x_ref, mod_ref, g_ref, w_ref, gq_ref, gk_ref, cos_ref, sin_ref,
                     q_ref, k_ref, v_ref, *, scale):
    h = _norm_mod(x_ref[...], g_ref[...], mod_ref[0:1, :], mod_ref[1:2, :]).astype(BF16)
    qkv = _dot(h, w_ref[...])
    cos = cos_ref[...]
    sin = sin_ref[...]
    nq = A_HEADS * A_HEAD_DIM // LANES
    nk = A_KV_HEADS * A_HEAD_DIM // LANES
    for c in range(nq):
        y = _half_rms(qkv[:, c * LANES:(c + 1) * LANES], gq_ref[...])
        y = _rope_chunk(y, cos, sin, A_HEAD_DIM // 4) * scale
        q_ref[:, c * LANES:(c + 1) * LANES] = y.astype(BF16)
    for c in range(nk):
        y = _half_rms(qkv[:, (nq + c) * LANES:(nq + c + 1) * LANES], gk_ref[...])
        y = _rope_chunk(y, cos, sin, A_HEAD_DIM // 4)
        a, b = _dup_halves(y)
        k_ref[2 * c] = a.astype(BF16)
        k_ref[2 * c + 1] = b.astype(BF16)
    for c in range(nk):
        a, b = _dup_halves(qkv[:, (nq + nk + c) * LANES:(nq + nk + c + 1) * LANES])
        v_ref[2 * c] = a.astype(BF16)
        v_ref[2 * c + 1] = b.astype(BF16)


def _gqa_proj(X, mod, g, w_qkv, gq, gk, cos, sin, geom):
    tm = geom.tm
    T = X.shape[0]
    nqkv = w_qkv.shape[1]
    kv_spec = pl.BlockSpec((A_KV_HEADS, tm, LANES), lambda i: (0, i, 0))
    tab_spec = pl.BlockSpec((tm, LANES), lambda i: geom.table_map(i))
    return pl.pallas_call(
        functools.partial(_gqa_proj_kernel, scale=A_HEAD_DIM ** -0.5),
        grid=(geom.n_all,),
        in_specs=[
            _row_spec(tm, D_MODEL), geom.mod_spec(), _full_spec((1, D_MODEL)),
            _full_spec((D_MODEL, nqkv)), _full_spec((1, LANES)), _full_spec((1, LANES)),
            tab_spec, tab_spec,
        ],
        out_specs=[_row_spec(tm, A_HEADS * A_HEAD_DIM), kv_spec, kv_spec],
        out_shape=[
            jax.ShapeDtypeStruct((T, A_HEADS * A_HEAD_DIM), BF16),
            jax.ShapeDtypeStruct((A_KV_HEADS, T, LANES), BF16),
            jax.ShapeDtypeStruct((A_KV_HEADS, T, LANES), BF16),
        ],
        compiler_params=_cparams(1),
        name="gqa_proj",
    )(X, mod, g, w_qkv, gq, gk, cos, sin)


def _mla_proj_kernel(x_ref, mod_ref, g_ref, wdn_ref, gcq_ref, gckv_ref, wuq_ref, wuk_ref,
                     wuv_ref, cos_ref, sin_ref, q_ref, k_ref, v_ref, *, scale):
    h = _norm_mod(x_ref[...], g_ref[...], mod_ref[0:1, :], mod_ref[1:2, :]).astype(BF16)
    d = _dot(h, wdn_ref[...])
    cos = cos_ref[...]
    sin = sin_ref[...]
    cq = _rms(d[:, :M_Q_LORA], gcq_ref[...]).astype(BF16)
    ckv = _rms(d[:, M_Q_LORA:M_Q_LORA + M_KV_LORA], gckv_ref[...]).astype(BF16)
    kr = _rope_chunk(d[:, M_Q_LORA + M_KV_LORA:], cos, sin, M_ROPE // 4)
    q = _dot(cq, wuq_ref[...])
    kn = _dot(ckv, wuk_ref[...])
    for c in range(M_HEADS):
        sl = slice(c * LANES, (c + 1) * LANES)
        q_ref[:, sl] = (_rope_chunk(q[:, sl], cos, sin, M_ROPE // 4) * scale).astype(BF16)
        k_ref[:, sl] = (kn[:, sl] + kr).astype(BF16)
    v_ref[...] = _dot(ckv, wuv_ref[...]).astype(BF16)


def _mla_proj(X, mod, g, wdn, gcq, gckv, wuq, wuk, wuv, cos, sin, geom):
    tm = geom.tm
    T = X.shape[0]
    tab_spec = pl.BlockSpec((tm, LANES), lambda i: geom.table_map(i))
    nq = M_HEADS * LANES
    nv = M_HEADS * M_V
    return pl.pallas_call(
        functools.partial(_mla_proj_kernel, scale=(M_NOPE + M_ROPE) ** -0.5),
        grid=(geom.n_all,),
        in_specs=[
            _row_spec(tm, D_MODEL), geom.mod_spec(), _full_spec((1, D_MODEL)),
            _full_spec(wdn.shape), _full_spec((1, M_Q_LORA)), _full_spec((1, M_KV_LORA)),
            _full_spec(wuq.shape), _full_spec(wuk.shape), _full_spec(wuv.shape),
            tab_spec, tab_spec,
        ],
        out_specs=[_row_spec(tm, nq), _row_spec(tm, nq), _row_spec(tm, nv)],
        out_shape=[
            jax.ShapeDtypeStruct((T, nq), BF16),
            jax.ShapeDtypeStruct((T, nq), BF16),
            jax.ShapeDtypeStruct((T, nv), BF16),
        ],
        compiler_params=_cparams(1),
        name="mla_proj",
    )(X, mod, g, wdn, gcq, gckv, wuq, wuk, wuv, cos, sin)


def _attn_kernel(q_ref, kl_ref, kc_ref, vl_ref, vc_ref, o_ref, m_ref, l_ref, acc_ref,
                 *, shared_k, tq, tk, n_lat_chunks, nq_lat):
    qt = pl.program_id(2)
    lo = lax.broadcasted_iota(jnp.int32, (tq, LANES), 1) < HALF
    q = q_ref[...]
    if shared_k:
        zero = jnp.zeros_like(q)
        qs = jnp.concatenate([jnp.where(lo, q, zero), jnp.where(lo, zero, q)], axis=0)

    nt = (((1,), (1,)), ((), ()))

    def scores(k):
        if shared_k:
            return lax.dot_general(qs, k, nt, preferred_element_type=F32)
        return jnp.concatenate([
            lax.dot_general(q[:, :LANES], k[:, :LANES], nt, preferred_element_type=F32),
            lax.dot_general(q[:, LANES:], k[:, LANES:], nt, preferred_element_type=F32),
        ], axis=0)

    m_ref[...] = jnp.full(m_ref.shape, -jnp.inf, F32)
    l_ref[...] = jnp.zeros(l_ref.shape, F32)
    acc_ref[...] = jnp.zeros(acc_ref.shape, F32)

    def step(k, v):
        s = scores(k)
        m_prev = m_ref[...]
        m_new = jnp.maximum(m_prev, jnp.max(s, axis=-1, keepdims=True))
        alpha = jnp.exp(m_prev - m_new)
        p = jnp.exp(s - m_new)
        l_ref[...] = alpha * l_ref[...] + jnp.sum(p, axis=-1, keepdims=True)
        acc_ref[...] = alpha * acc_ref[...] + _dot(p.astype(BF16), v)
        m_ref[...] = m_new

    step(kc_ref[...], vc_ref[...])

    def body(j, carry):
        off = pl.multiple_of(j * tk, tk)
        step(kl_ref[pl.ds(off, tk), :], vl_ref[pl.ds(off, tk), :])
        return carry

    lax.fori_loop(0, jnp.where(qt < nq_lat, n_lat_chunks, 0), body, 0)
    o = acc_ref[...] / l_ref[...]
    o_ref[...] = jnp.where(lo, o[:tq], o[tq:]).astype(o_ref.dtype)


def _attention(q, k, v, *, B, S, L, shared_k, ctx_queries):
    T = q.shape[0]
    tq = L
    tk = min(512, S)
    units = 8
    nq_lat = S // tq
    nq = nq_lat + (1 if ctx_queries else 0)
    ctx_blk0 = B * S // L

    def qrow(b, qt):
        return jnp.where(qt < nq_lat, b * nq_lat + qt, ctx_blk0 + b)

    if shared_k:
        q_spec = pl.BlockSpec((tq, LANES), lambda b, p, qt: (qrow(b, qt), p))
        kl_spec = pl.BlockSpec((None, S, LANES), lambda b, p, qt: (p // 2, b, 0))
        kc_spec = pl.BlockSpec((None, L, LANES), lambda b, p, qt: (p // 2, ctx_blk0 + b, 0))
        vl_spec, vc_spec = kl_spec, kc_spec
    else:
        q_spec = pl.BlockSpec((tq, 2 * LANES), lambda b, p, qt: (qrow(b, qt), p))
        kl_spec = pl.BlockSpec((S, 2 * LANES), lambda b, p, qt: (b, p))
        kc_spec = pl.BlockSpec((L, 2 * LANES), lambda b, p, qt: (ctx_blk0 + b, p))
        vl_spec = pl.BlockSpec((S, LANES), lambda b, p, qt: (b, p))
        vc_spec = pl.BlockSpec((L, LANES), lambda b, p, qt: (ctx_blk0 + b, p))
    return pl.pallas_call(
        functools.partial(_attn_kernel, shared_k=shared_k, tq=tq, tk=tk,
                          n_lat_chunks=S // tk, nq_lat=nq_lat),
        grid=(B, units, nq),
        in_specs=[q_spec, kl_spec, kc_spec, vl_spec, vc_spec],
        out_specs=pl.BlockSpec((tq, LANES), lambda b, p, qt: (qrow(b, qt), p)),
        out_shape=jax.ShapeDtypeStruct((T if ctx_queries else B * S, units * LANES), BF16),
        scratch_shapes=[
            pltpu.VMEM((2 * tq, 1), F32), pltpu.VMEM((2 * tq, 1), F32),
            pltpu.VMEM((2 * tq, LANES), F32),
        ],
        compiler_params=_cparams(3),
        name="attention",
    )(q, k, k, v, v)


def _linear_resid_kernel(a_ref, w_ref, x_ref, mod_ref, g_ref, o_ref, *, gate_row):
    y = _dot(a_ref[...], w_ref[...])
    o_ref[...] = x_ref[...] + mod_ref[gate_row:gate_row + 1, :] * _rms(y, g_ref[...])


def _linear_resid(A, W, X, mod, g, geom, *, gate_row, n_tiles):
    tm = geom.tm
    return pl.pallas_call(
        functools.partial(_linear_resid_kernel, gate_row=gate_row),
        grid=(n_tiles,),
        in_specs=[
            _row_spec(tm, A.shape[1]), _full_spec(W.shape), _row_spec(tm, D_MODEL),
            geom.mod_spec(), _full_spec((1, D_MODEL)),
        ],
        out_specs=_row_spec(tm, D_MODEL),
        out_shape=jax.ShapeDtypeStruct((n_tiles * tm, D_MODEL), F32),
        compiler_params=_cparams(1),
        name="linear_resid",
    )(A, W, X, mod, g)


def _norm_linear_kernel(x_ref, mod_ref, g_ref, w_ref, o_ref):
    h = _norm_mod(x_ref[...], g_ref[...], mod_ref[0:1, :], mod_ref[1:2, :]).astype(BF16)
    o_ref[...] = _dot(h, w_ref[...]).astype(o_ref.dtype)


def _norm_linear(X, mod, g, W, geom):
    tm = geom.tm
    n = W.shape[1]
    return pl.pallas_call(
        _norm_linear_kernel,
        grid=(geom.n_all,),
        in_specs=[_row_spec(tm, D_MODEL), geom.mod_spec(), _full_spec((1, D_MODEL)),
                  _full_spec(W.shape)],
        out_specs=_row_spec(tm, n),
        out_shape=jax.ShapeDtypeStruct((X.shape[0], n), BF16),
        compiler_params=_cparams(1),
        name="norm_linear",
    )(X, mod, g, W)


def _seq_dft_kernel(c_ref, s_ref, y1_ref, y2_ref, o_ref, acc_ref):
    k = pl.program_id(2)

    @pl.when(k == 0)
    def _():
        acc_ref[...] = jnp.zeros(acc_ref.shape, F32)

    acc_ref[...] += _dot(c_ref[...], y1_ref[...]) + _dot(s_ref[...], y2_ref[...])

    @pl.when(k == pl.num_programs(2) - 1)
    def _():
        o_ref[...] = acc_ref[...].astype(o_ref.dtype)


def _seq_dft(cmat, smat, Y, *, n_batch, n_seq, row_blk0):
    tu = min(1024, n_seq)
    nu = n_seq // tu
    return pl.pallas_call(
        _seq_dft_kernel,
        grid=(n_batch, nu, nu),
        in_specs=[
            pl.BlockSpec((tu, tu), lambda b, u, k: (u, k)),
            pl.BlockSpec((tu, tu), lambda b, u, k: (u, k)),
            pl.BlockSpec((tu, D_MODEL), lambda b, u, k: (row_blk0 + b * nu + k, 0)),
            pl.BlockSpec((tu, D_MODEL), lambda b, u, k: (row_blk0 + b * nu + k, 1)),
        ],
        out_specs=pl.BlockSpec((tu, D_MODEL), lambda b, u, k: (b * nu + u, 0)),
        out_shape=jax.ShapeDtypeStruct((n_batch * n_seq, D_MODEL), BF16),
        scratch_shapes=[pltpu.VMEM((tu, D_MODEL), F32)],
        compiler_params=_cparams(3),
        name="seq_dft",
    )(cmat, smat, Y, Y)


def _swiglu_chunks(h, wgu_ref, wd_ref, acc_ref, ff, tf):
    for c in range(ff // tf):
        g = _dot(h, wgu_ref[:, c * tf:(c + 1) * tf])
        u = _dot(h, wgu_ref[:, ff + c * tf:ff + (c + 1) * tf])
        a = ((g * jax.nn.sigmoid(g)) * u).astype(BF16)
        part = _dot(a, wd_ref[c * tf:(c + 1) * tf, :])
        if c == 0:
            acc_ref[...] = part
        else:
            acc_ref[...] += part


def _ffn_dense_kernel(x_ref, mod_ref, g2_ref, g3_ref, wgu_ref, wd_ref, o_ref, acc_ref,
                      *, ff, tf):
    x = x_ref[...]
    h = _norm_mod(x, g2_ref[...], mod_ref[3:4, :], mod_ref[4:5, :]).astype(BF16)
    _swiglu_chunks(h, wgu_ref, wd_ref, acc_ref, ff, tf)
    o_ref[...] = x + mod_ref[5:6, :] * _rms(acc_ref[...], g3_ref[...])


def _ffn_dense(X, mod, g2, g3, wgu, wd, geom):
    tm = geom.tm
    ff = wd.shape[0]
    single = pl.Buffered(1)
    return pl.pallas_call(
        functools.partial(_ffn_dense_kernel, ff=ff, tf=256),
        grid=(geom.n_all,),
        in_specs=[
            _row_spec(tm, D_MODEL), geom.mod_spec(), _full_spec((1, D_MODEL)),
            _full_spec((1, D_MODEL)),
            pl.BlockSpec(wgu.shape, lambda i: (0, 0), pipeline_mode=single),
            pl.BlockSpec(wd.shape, lambda i: (0, 0), pipeline_mode=single),
        ],
        out_specs=_row_spec(tm, D_MODEL),
        out_shape=jax.ShapeDtypeStruct(X.shape, F32),
        scratch_shapes=[pltpu.VMEM((tm, D_MODEL), F32)],
        compiler_params=_cparams(1),
        name="ffn_dense",
    )(X, mod, g2, g3, wgu, wd)


def _router_kernel(x_ref, mod_ref, g_ref, wr_ref, h_ref, r_ref):
    h = _norm_mod(x_ref[...], g_ref[...], mod_ref[3:4, :], mod_ref[4:5, :])
    h_ref[...] = h
    logits = jnp.dot(h, wr_ref[...], precision=lax.Precision.HIGHEST,
                     preferred_element_type=F32)
    lane = lax.broadcasted_iota(jnp.int32, logits.shape, 1)
    lane_f = lane.astype(F32)
    neg = jnp.float32(-jnp.inf)
    logits = jnp.where(lane < N_EXPERTS, logits, neg)
    v1 = jnp.max(logits, axis=-1, keepdims=True)
    i1 = jnp.min(jnp.where(logits == v1, lane_f, float(LANES)), axis=-1, keepdims=True)
    rest = jnp.where(lane_f == i1, neg, logits)
    v2 = jnp.max(rest, axis=-1, keepdims=True)
    i2 = jnp.min(jnp.where(rest == v2, lane_f, float(LANES)), axis=-1, keepdims=True)
    e2 = jnp.exp(v2 - v1)
    den = 1.0 + e2
    w1 = 1.0 / den
    w2 = e2 / den
    out = jnp.where(lane == 0, w1, 0.0)
    out = jnp.where(lane == 1, w2, out)
    out = jnp.where(lane == 2, i1, out)
    out = jnp.where(lane == 3, i2, out)
    r_ref[...] = out


def _router(X, mod, g, wr, geom, n_tiles):
    tm = geom.tm
    rows = n_tiles * tm
    return pl.pallas_call(
        _router_kernel,
        grid=(n_tiles,),
        in_specs=[_row_spec(tm, D_MODEL), geom.mod_spec(), _full_spec((1, D_MODEL)),
                  _full_spec(wr.shape)],
        out_specs=[_row_spec(tm, D_MODEL), _row_spec(tm, LANES)],
        out_shape=[jax.ShapeDtypeStruct((rows, D_MODEL), F32),
                   jax.ShapeDtypeStruct((rows, LANES), F32)],
        compiler_params=_cparams(1),
        name="moe_router",
    )(X, mod, g, wr)


def _start_row_gather(src_hbm, idx_ref, dst, sem, n_rows):
    def body(r, carry):
        t = idx_ref[0, 0, r]
        pltpu.make_async_copy(src_hbm.at[pl.ds(t, 1), :], dst.at[pl.ds(r, 1), :], sem).start()
        return carry

    lax.fori_loop(0, n_rows, body, 0)


def _wait_row_gather(src_hbm, dst, sem, n_rows):
    pltpu.make_async_copy(src_hbm.at[pl.ds(0, n_rows), :], dst, sem).wait()


def _moe_ffn_kernel(te_ref, nu_ref, idx_ref, idx_next_ref, h_hbm, wgu_ref, wd_ref, o_ref,
                    gbuf, sem, acc_ref, *, tm, ff, tf):
    i = pl.program_id(0)
    n_used = nu_ref[0]
    slot = i % 2

    @pl.when(i == 0)
    def _():
        _start_row_gather(h_hbm, idx_ref, gbuf.at[0], sem.at[0], tm)

    @pl.when(i < n_used)
    def _():
        _wait_row_gather(h_hbm, gbuf.at[slot], sem.at[slot], tm)

        @pl.when(i + 1 < n_used)
        def _():
            _start_row_gather(h_hbm, idx_next_ref, gbuf.at[1 - slot], sem.at[1 - slot], tm)

        h = gbuf[slot].astype(BF16)
        _swiglu_chunks(h, wgu_ref, wd_ref, acc_ref, ff, tf)
        o_ref[...] = acc_ref[...]

    @pl.when(i >= n_used)
    def _():
        o_ref[...] = jnp.zeros(o_ref.shape, o_ref.dtype)


def _moe_ffn(h2, wgu, wd, tile_expert, n_used, slot_token, *, tm, n_tiles):
    ff = wd.shape[1]
    single = pl.Buffered(1)
    grid_spec = pltpu.PrefetchScalarGridSpec(
        num_scalar_prefetch=2,
        grid=(n_tiles,),
        in_specs=[
            pl.BlockSpec((1, 1, tm), lambda i, te, nu: (i, 0, 0), memory_space=pltpu.SMEM),
            pl.BlockSpec((1, 1, tm), lambda i, te, nu: (jnp.minimum(i + 1, n_tiles - 1), 0, 0),
                         memory_space=pltpu.SMEM),
            pl.BlockSpec(memory_space=pl.ANY),
            pl.BlockSpec((None, D_MODEL, 2 * ff), lambda i, te, nu: (te[i], 0, 0),
                         pipeline_mode=single),
            pl.BlockSpec((None, ff, D_MODEL), lambda i, te, nu: (te[i], 0, 0),
                         pipeline_mode=single),
        ],
        out_specs=pl.BlockSpec((tm, D_MODEL), lambda i, te, nu: (i, 0)),
        scratch_shapes=[
            pltpu.VMEM((2, tm, D_MODEL), F32),
            pltpu.SemaphoreType.DMA((2,)),
            pltpu.VMEM((tm, D_MODEL), F32),
        ],
    )
    return pl.pallas_call(
        functools.partial(_moe_ffn_kernel, tm=tm, ff=ff, tf=512),
        grid_spec=grid_spec,
        out_shape=jax.ShapeDtypeStruct((n_tiles * tm, D_MODEL), F32),
        compiler_params=_cparams(1),
        name="moe_ffn",
    )(tile_expert, n_used, slot_token, slot_token, h2, wgu, wd)


def _moe_combine_kernel(idx_ref, idx_next_ref, y_hbm, r_ref, x_ref, mod_ref, g_ref, o_ref,
                        gbuf, sem, *, tm):
    i = pl.program_id(0)
    n = pl.num_programs(0)
    slot = i % 2

    @pl.when(i == 0)
    def _():
        _start_row_gather(y_hbm, idx_ref, gbuf.at[0], sem.at[0], 2 * tm)

    _wait_row_gather(y_hbm, gbuf.at[slot], sem.at[slot], 2 * tm)

    @pl.when(i + 1 < n)
    def _():
        _start_row_gather(y_hbm, idx_next_ref, gbuf.at[1 - slot], sem.at[1 - slot], 2 * tm)

    r = r_ref[...]
    y = r[:, 0:1] * gbuf[slot, 0:tm, :] + r[:, 1:2] * gbuf[slot, tm:2 * tm, :]
    o_ref[...] = x_ref[...] + mod_ref[5:6, :] * _rms(y, g_ref[...])


def _moe_combine(Y, tok_slot, route, X, mod, g, geom, n_tiles):
    tm = geom.tm
    return pl.pallas_call(
        functools.partial(_moe_combine_kernel, tm=tm),
        grid=(n_tiles,),
        in_specs=[
            pl.BlockSpec((1, 1, 2 * tm), lambda i: (i, 0, 0), memory_space=pltpu.SMEM),
            pl.BlockSpec((1, 1, 2 * tm), lambda i: (jnp.minimum(i + 1, n_tiles - 1), 0, 0),
                         memory_space=pltpu.SMEM),
            pl.BlockSpec(memory_space=pl.ANY),
            _row_spec(tm, LANES), _row_spec(tm, D_MODEL), geom.mod_spec(),
            _full_spec((1, D_MODEL)),
        ],
        out_specs=_row_spec(tm, D_MODEL),
        out_shape=jax.ShapeDtypeStruct((n_tiles * tm, D_MODEL), F32),
        scratch_shapes=[pltpu.VMEM((2, 2 * tm, D_MODEL), F32), pltpu.SemaphoreType.DMA((2,))],
        compiler_params=_cparams(1),
        name="moe_combine",
    )(tok_slot, tok_slot, Y, route, X, mod, g)


def _moe_layer(X, mod, g2, g3, w_router, wgu, wd, geom, n_tiles, tm_e):
    tm = geom.tm
    rows = n_tiles * tm
    wr = jnp.pad(w_router, ((0, 0), (0, LANES - N_EXPERTS)))
    h2, route = _router(X, mod, g2, wr, geom, n_tiles)

    e_flat = route[:, 2:4].astype(jnp.int32).reshape(-1)
    onehot = (e_flat[:, None] == jnp.arange(N_EXPERTS, dtype=jnp.int32)[None, :]).astype(jnp.int32)
    csum = jnp.cumsum(onehot, axis=0)
    rank = jnp.sum(csum * onehot, axis=1) - 1
    counts = csum[-1]
    tiles_e = (counts + tm_e - 1) // tm_e
    tile_end = jnp.cumsum(tiles_e)
    tile_start = tile_end - tiles_e
    n_used = tile_end[-1]
    slot = tile_start[e_flat] * tm_e + rank
    n_tiles_e = (2 * rows) // tm_e + N_EXPERTS
    tok = jnp.arange(2 * rows, dtype=jnp.int32) // 2
    slot_token = jnp.zeros((n_tiles_e * tm_e,), jnp.int32).at[slot].set(tok)
    tile_ids = jnp.minimum(jnp.arange(n_tiles_e, dtype=jnp.int32), n_used - 1)
    tile_expert = jnp.sum((tile_ids[:, None] >= tile_end[None, :]).astype(jnp.int32), axis=1)
    tile_expert = jnp.minimum(tile_expert, N_EXPERTS - 1).astype(jnp.int32)

    Y = _moe_ffn(h2, wgu, wd, tile_expert, n_used.reshape(1).astype(jnp.int32),
                 slot_token.reshape(n_tiles_e, 1, tm_e), tm=tm_e, n_tiles=n_tiles_e)
    slot2 = slot.reshape(rows, 2)
    tok_slot = jnp.concatenate(
        [slot2[:, 0].reshape(n_tiles, 1, tm), slot2[:, 1].reshape(n_tiles, 1, tm)], axis=2)
    return _moe_combine(Y, tok_slot.astype(jnp.int32), route, X, mod, g3, geom, n_tiles)


def _rope_tables(S, tm, rot_dim, lane0, reps):
    nf = rot_dim // 4
    t = jnp.arange(S, dtype=jnp.int32)
    rows = (t // GRID_W).astype(F32)
    cols = (t % GRID_W).astype(F32)
    inv = ROPE_BASE ** (-jnp.arange(nf, dtype=F32) / nf)
    ang_r = rows[:, None] * inv[None, :]
    ang_c = cols[:, None] * inv[None, :]
    cr, sr, cc, sc = jnp.cos(ang_r), jnp.sin(ang_r), jnp.cos(ang_c), jnp.sin(ang_c)
    cos_blk = jnp.concatenate([cr, cr, cc, cc], axis=1)
    sin_blk = jnp.concatenate([-sr, sr, -sc, sc], axis=1)
    cos = jnp.ones((S, LANES), F32)
    sin = jnp.zeros((S, LANES), F32)
    for r in range(reps):
        a = lane0 + r * rot_dim
        cos = cos.at[:, a:a + rot_dim].set(cos_blk)
        sin = sin.at[:, a:a + rot_dim].set(sin_blk)
    cos = jnp.concatenate([cos, jnp.ones((tm, LANES), F32)], axis=0)
    sin = jnp.concatenate([sin, jnp.zeros((tm, LANES), F32)], axis=0)
    return cos, sin


def _dft_mats(n, scale):
    u = jnp.arange(n, dtype=jnp.int32)
    idx = (u[:, None] * u[None, :]) % n
    ang = idx.astype(F32) * (2.0 * math.pi / n)
    return (jnp.cos(ang) * scale).astype(BF16), (-jnp.sin(ang) * scale).astype(BF16)


def _channel_dft_weight():
    n = D_MODEL // F_GROUPS
    v = jnp.arange(n, dtype=jnp.int32)
    ang = ((v[:, None] * v[None, :]) % n).astype(F32) * (2.0 * math.pi / n)
    eye = jnp.eye(F_GROUPS, dtype=F32)
    wc = jnp.kron(eye, jnp.cos(ang))
    ws = jnp.kron(eye, jnp.sin(ang))
    return jnp.concatenate([wc, ws], axis=1).astype(BF16)


def kernel(x, c, ctx, c_ctx, w_mod, b_mod, norm_g, a_w_qkv, a_g_q, a_g_k, a_w_o, f_w,
           m_w_down, m_g_cq, m_g_ckv, m_w_uq, m_w_ukv, m_w_o, d_w_gu, d_w_down,
           e_w_router, e_w_gu, e_w_down):
    B, S, D = x.shape
    L = ctx.shape[1]
    depth = w_mod.shape[0]
    assert D == D_MODEL and B + 1 <= N_MOD_ROWS and S % L == 0
    tm = min(512, S)
    geom = _Geom(B, S, L, tm)
    n_lat, n_all = geom.n_lat, geom.n_all

    X = jnp.concatenate([x.reshape(B * S, D), ctx.reshape(B * L, D)], axis=0)
    cvec = jnp.zeros((N_MOD_ROWS, D), F32).at[:B].set(c).at[B].set(c_ctx)
    mods = _mod_vectors(cvec, w_mod, b_mod)

    rope_a = _rope_tables(S, tm, A_HEAD_DIM, 0, 2)
    rope_m = _rope_tables(S, tm, M_ROPE, M_NOPE, 1)

    for i in range(depth):
        last = i == depth - 1
        kind = i % N_MIXERS
        j = i // N_MIXERS
        mod = mods[i]
        g = norm_g[i].reshape(4, 1, D)
        n_out = n_lat if last else n_all

        if kind == 0:
            gq = jnp.tile(a_g_q[j], 2).reshape(1, LANES)
            gk = jnp.tile(a_g_k[j], 2).reshape(1, LANES)
            q, kd, vd = _gqa_proj(X, mod, g[0], a_w_qkv[j].astype(BF16), gq, gk,
                                  rope_a[0], rope_a[1], geom)
            att = _attention(q, kd, vd, B=B, S=S, L=L, shared_k=True, ctx_queries=not last)
            X = _linear_resid(att, a_w_o[j].astype(BF16), X, mod, g[1], geom,
                              gate_row=2, n_tiles=n_out)
        elif kind == 1:
            Y = _norm_linear(X, mod, g[0], _channel_dft_weight(), geom)
            n_ch = D // F_GROUPS
            cl, sl = _dft_mats(S, 1.0 / math.sqrt(S * n_ch))
            Z = _seq_dft(cl, sl, Y, n_batch=B, n_seq=S, row_blk0=0)
            if not last:
                cc_, sc_ = _dft_mats(L, 1.0 / math.sqrt(L * n_ch))
                Zc = _seq_dft(cc_, sc_, Y, n_batch=B, n_seq=L, row_blk0=B * S // L)
                Z = jnp.concatenate([Z, Zc], axis=0)
            X = _linear_resid(Z, f_w[j].astype(BF16), X, mod, g[1], geom,
                              gate_row=2, n_tiles=n_out)
        else:
            wd = m_w_down[j]
            wdn = jnp.concatenate([
                wd[:, :M_Q_LORA + M_KV_LORA],
                jnp.zeros((D, M_NOPE), F32), wd[:, M_Q_LORA + M_KV_LORA:],
                jnp.zeros((D, LANES - M_NOPE - M_ROPE), F32)], axis=1).astype(BF16)
            wuq = jnp.pad(m_w_uq[j].reshape(M_Q_LORA, M_HEADS, M_NOPE + M_ROPE),
                          ((0, 0), (0, 0), (0, LANES - M_NOPE - M_ROPE)))
            wuq = wuq.reshape(M_Q_LORA, M_HEADS * LANES).astype(BF16)
            wukv = m_w_ukv[j].reshape(M_KV_LORA, M_HEADS, M_NOPE + M_V)
            wuk = jnp.pad(wukv[:, :, :M_NOPE], ((0, 0), (0, 0), (0, LANES - M_NOPE)))
            wuk = wuk.reshape(M_KV_LORA, M_HEADS * LANES).astype(BF16)
            wuv = wukv[:, :, M_NOPE:].reshape(M_KV_LORA, M_HEADS * M_V).astype(BF16)
            q, k, v = _mla_proj(X, mod, g[0], wdn, m_g_cq[j].reshape(1, -1),
                                m_g_ckv[j].reshape(1, -1), wuq, wuk, wuv,
                                rope_m[0], rope_m[1], geom)
            att = _attention(q, k, v, B=B, S=S, L=L, shared_k=False, ctx_queries=not last)
            X = _linear_resid(att, m_w_o[j].astype(BF16), X, mod, g[1], geom,
                              gate_row=2, n_tiles=n_out)

        if i % 2 == 0:
            assert not last
            X = _ffn_dense(X, mod, g[2], g[3], d_w_gu[i // 2].astype(BF16),
                           d_w_down[i // 2].astype(BF16), geom)
        else:
            X = _moe_layer(X, mod, g[2], g[3], e_w_router[i // 2],
                           e_w_gu[i // 2].astype(BF16), e_w_down[i // 2].astype(BF16),
                           geom, n_out, min(512, S))

    return X[:B * S].reshape(B, S, D)
```

```python
import functools
import math

import jax
import jax.numpy as jnp
from jax import lax
from jax.experimental import pallas as pl
from jax.experimental.pallas import tpu as pltpu

F32 = jnp.float32
BF16 = jnp.bfloat16

D_MODEL = 1024
GRID_W = 64
NORM_EPS = 1e-6
ROPE_BASE = 10000.0
LANES = 128
HALF = LANES // 2

A_HEADS = 16
A_KV_HEADS = 4
A_HEAD_DIM = 64
F_GROUPS = 4
M_HEADS = 16
M_Q_LORA = 384
M_KV_LORA = 256
M_NOPE = 64
M_ROPE = 32
M_V = 64
N_EXPERTS = 8
N_MIXERS = 3

VT_ROWS = LANES + 16
LOG2_E = math.log2(math.e)
VMEM_LIMIT = 56 * 1024 * 1024
N_MOD_ROWS = 16


def _cparams(n_axes):
    return pltpu.CompilerParams(
        dimension_semantics=("arbitrary",) * n_axes, vmem_limit_bytes=VMEM_LIMIT)


def _rms(x, g):
    y = x * lax.rsqrt(jnp.mean(x * x, axis=-1, keepdims=True) + NORM_EPS)
    return y * g


def _norm_mod(x, g, shift, scale):
    return _rms(x, g) * (1.0 + scale) + shift


def _half_rms(y, g):
    lo = lax.broadcasted_iota(jnp.int32, y.shape, 1) < HALF
    s = y * y
    s_lo = jnp.sum(jnp.where(lo, s, 0.0), axis=-1, keepdims=True)
    s_hi = jnp.sum(jnp.where(lo, 0.0, s), axis=-1, keepdims=True)
    ms = jnp.where(lo, s_lo, s_hi) * (1.0 / HALF)
    return (y * lax.rsqrt(ms + NORM_EPS)) * g


def _rope_chunk(y, cos, sin, width):
    lane = lax.broadcasted_iota(jnp.int32, y.shape, 1)
    first = ((lane // width) % 2) == 0
    partner = jnp.where(first, pltpu.roll(y, LANES - width, 1), pltpu.roll(y, width, 1))
    return y * cos + partner * sin


def _dup_halves(y):
    lo = lax.broadcasted_iota(jnp.int32, y.shape, 1) < HALF
    r = pltpu.roll(y, HALF, 1)
    return jnp.where(lo, y, r), jnp.where(lo, r, y)


def _dot(a, b):
    return jnp.dot(a, b, preferred_element_type=F32)


def _mod_kernel(c_ref, w_ref, b_ref, o_ref):
    c = c_ref[...]
    sc = c * jax.nn.sigmoid(c)
    o_ref[...] = jnp.dot(sc, w_ref[...], precision=lax.Precision.HIGHEST,
                         preferred_element_type=F32) + b_ref[...]


def _mod_vectors(cvec, w_mod, b_mod):
    depth, d, n = w_mod.shape
    tn = 1536
    out = pl.pallas_call(
        _mod_kernel,
        grid=(depth, n // tn),
        in_specs=[
            pl.BlockSpec((N_MOD_ROWS, d), lambda i, j: (0, 0)),
            pl.BlockSpec((None, d, tn), lambda i, j: (i, 0, j)),
            pl.BlockSpec((None, 1, tn), lambda i, j: (i, 0, j)),
        ],
        out_specs=pl.BlockSpec((None, N_MOD_ROWS, tn), lambda i, j: (i, 0, j)),
        out_shape=jax.ShapeDtypeStruct((depth, N_MOD_ROWS, n), F32),
        compiler_params=_cparams(2),
        name="mod_vectors",
    )(cvec, w_mod, b_mod.reshape(depth, 1, n))
    return out.reshape(depth, N_MOD_ROWS, 6, d)


class _Geom:
    def __init__(self, B, S, L, tm):
        assert S % tm == 0 and (B * L) % tm == 0
        self.B, self.S, self.L, self.tm = B, S, L, tm
        self.n_lat = B * S // tm
        self.n_all = (B * S + B * L) // tm
        self.per_batch = S // tm

    def mod_map(self, i):
        return (jnp.where(i < self.n_lat, i // self.per_batch, self.B), 0, 0)

    def table_map(self, i):
        return (jnp.where(i < self.n_lat, i % self.per_batch, self.per_batch), 0)

    def mod_spec(self):
        return pl.BlockSpec((None, 6, D_MODEL), lambda i: self.mod_map(i))


def _row_spec(tm, n):
    return pl.BlockSpec((tm, n), lambda i: (i, 0))


def _full_spec(shape):
    nd = len(shape)
    return pl.BlockSpec(shape, lambda i: (0,) * nd)


def _gqa_proj_kernel(x_ref, mod_ref, g_ref, w_ref, gq_ref, gk_ref, cos_ref, sin_ref,
                     q_ref, k_ref, vt_ref, *, scale):
    h = _norm_mod(x_ref[...], g_ref[...], mod_ref[0:1, :], mod_ref[1:2, :]).astype(BF16)
    qkv = _dot(h, w_ref[...])
    cos = cos_ref[...]
    sin = sin_ref[...]
    nq = A_HEADS * A_HEAD_DIM // LANES
    nk = A_KV_HEADS * A_HEAD_DIM // LANES
    for c in range(nq):
        y = _half_rms(qkv[:, c * LANES:(c + 1) * LANES], gq_ref[...])
        y = _rope_chunk(y, cos, sin, A_HEAD_DIM // 4) * scale
        q_ref[:, c * LANES:(c + 1) * LANES] = y.astype(BF16)
    for c in range(nk):
        y = _half_rms(qkv[:, (nq + c) * LANES:(nq + c + 1) * LANES], gk_ref[...])
        y = _rope_chunk(y, cos, sin, A_HEAD_DIM // 4)
        a, b = _dup_halves(y)
        k_ref[2 * c] = a.astype(BF16)
        k_ref[2 * c + 1] = b.astype(BF16)
    for c in range(nk):
        a, b = _dup_halves(qkv[:, (nq + nk + c) * LANES:(nq + nk + c + 1) * LANES])
        vt_ref[2 * c, :LANES, :] = a.T.astype(BF16)
        vt_ref[2 * c + 1, :LANES, :] = b.T.astype(BF16)
    vt_ref[:, LANES:, :] = jnp.ones((A_KV_HEADS, VT_ROWS - LANES, vt_ref.shape[2]), BF16)


def _gqa_proj(X, mod, g, w_qkv, gq, gk, cos, sin, geom):
    tm = geom.tm
    T = X.shape[0]
    nqkv = w_qkv.shape[1]
    kv_spec = pl.BlockSpec((A_KV_HEADS, tm, LANES), lambda i: (0, i, 0))
    tab_spec = pl.BlockSpec((tm, LANES), lambda i: geom.table_map(i))
    return pl.pallas_call(
        functools.partial(_gqa_proj_kernel, scale=A_HEAD_DIM ** -0.5 * LOG2_E),
        grid=(geom.n_all,),
        in_specs=[
            _row_spec(tm, D_MODEL), geom.mod_spec(), _full_spec((1, D_MODEL)),
            _full_spec((D_MODEL, nqkv)), _full_spec((1, LANES)), _full_spec((1, LANES)),
            tab_spec, tab_spec,
        ],
        out_specs=[_row_spec(tm, A_HEADS * A_HEAD_DIM), kv_spec,
                   pl.BlockSpec((A_KV_HEADS, VT_ROWS, tm), lambda i: (0, 0, i))],
        out_shape=[
            jax.ShapeDtypeStruct((T, A_HEADS * A_HEAD_DIM), BF16),
            jax.ShapeDtypeStruct((A_KV_HEADS, T, LANES), BF16),
            jax.ShapeDtypeStruct((A_KV_HEADS, VT_ROWS, T), BF16),
        ],
        compiler_params=_cparams(1),
        name="gqa_proj",
    )(X, mod, g, w_qkv, gq, gk, cos, sin)


def _mla_proj_kernel(x_ref, mod_ref, g_ref, wdn_ref, gcq_ref, gckv_ref, wuq_ref, wuk_ref,
                     wuv_ref, cos_ref, sin_ref, q_ref, k_ref, vt_ref, *, scale):
    h = _norm_mod(x_ref[...], g_ref[...], mod_ref[0:1, :], mod_ref[1:2, :]).astype(BF16)
    d = _dot(h, wdn_ref[...])
    cos = cos_ref[...]
    sin = sin_ref[...]
    cq = _rms(d[:, :M_Q_LORA], gcq_ref[...]).astype(BF16)
    ckv = _rms(d[:, M_Q_LORA:M_Q_LORA + M_KV_LORA], gckv_ref[...]).astype(BF16)
    kr = _rope_chunk(d[:, M_Q_LORA + M_KV_LORA:], cos, sin, M_ROPE // 4)
    q = _dot(cq, wuq_ref[...])
    kn = _dot(ckv, wuk_ref[...])
    for c in range(M_HEADS):
        sl = slice(c * LANES, (c + 1) * LANES)
        q_ref[:, sl] = (_rope_chunk(q[:, sl], cos, sin, M_ROPE // 4) * scale).astype(BF16)
        k_ref[:, sl] = (kn[:, sl] + kr).astype(BF16)
    v = _dot(ckv, wuv_ref[...])
    nv = M_HEADS * M_V
    for c in range(nv // LANES):
        sl = slice(c * LANES, (c + 1) * LANES)
        vt_ref[c, :LANES, :] = v[:, sl].T.astype(BF16)
    vt_ref[:, LANES:, :] = jnp.ones((nv // LANES, VT_ROWS - LANES, vt_ref.shape[2]), BF16)


def _mla_proj(X, mod, g, wdn, gcq, gckv, wuq, wuk, wuv, cos, sin, geom):
    tm = geom.tm
    T = X.shape[0]
    tab_spec = pl.BlockSpec((tm, LANES), lambda i: geom.table_map(i))
    nq = M_HEADS * LANES
    nv = M_HEADS * M_V
    return pl.pallas_call(
        functools.partial(_mla_proj_kernel, scale=(M_NOPE + M_ROPE) ** -0.5 * LOG2_E),
        grid=(geom.n_all,),
        in_specs=[
            _row_spec(tm, D_MODEL), geom.mod_spec(), _full_spec((1, D_MODEL)),
            _full_spec(wdn.shape), _full_spec((1, M_Q_LORA)), _full_spec((1, M_KV_LORA)),
            _full_spec(wuq.shape), _full_spec(wuk.shape), _full_spec(wuv.shape),
            tab_spec, tab_spec,
        ],
        out_specs=[_row_spec(tm, nq), _row_spec(tm, nq),
                   pl.BlockSpec((nv // LANES, VT_ROWS, tm), lambda i: (0, 0, i))],
        out_shape=[
            jax.ShapeDtypeStruct((T, nq), BF16),
            jax.ShapeDtypeStruct((T, nq), BF16),
            jax.ShapeDtypeStruct((nv // LANES, VT_ROWS, T), BF16),
        ],
        compiler_params=_cparams(1),
        name="mla_proj",
    )(X, mod, g, wdn, gcq, gckv, wuq, wuk, wuv, cos, sin)


def _attn_kernel(q_ref, kl_ref, kc_ref, vtl_ref, vtc_ref, o_ref, acc_ref,
                 *, shared_k, tq, tk, n_lat_chunks, nq_lat, ctx_queries):
    qt = pl.program_id(2)
    q = q_ref[...]
    if shared_k:
        lo = lax.broadcasted_iota(jnp.int32, q.shape, 1) < HALF
        zero = jnp.zeros_like(q)
        qs = jnp.concatenate([jnp.where(lo, q, zero), jnp.where(lo, zero, q)], axis=0)

    nt = (((1,), (1,)), ((), ()))

    def scores_t(k):
        if shared_k:
            return lax.dot_general(k, qs, nt, preferred_element_type=F32)
        return jnp.concatenate([
            lax.dot_general(k[:, :LANES], q[:, :LANES], nt, preferred_element_type=F32),
            lax.dot_general(k[:, LANES:], q[:, LANES:], nt, preferred_element_type=F32),
        ], axis=1)

    def soft_pv(s, vt, m_prev):
        m_cur = jnp.max(s, axis=0, keepdims=True)
        if m_prev is None:
            p = jnp.exp2(s - m_cur)
            acc_ref[...] = _dot(vt, p.astype(BF16))
            return m_cur
        m_new = jnp.maximum(m_prev, m_cur)
        alpha = jnp.exp2(m_prev - m_new)
        p = jnp.exp2(s - m_new)
        acc_ref[...] = alpha * acc_ref[...] + _dot(vt, p.astype(BF16))
        return m_new

    def run(chunks):
        s_next = scores_t(chunks[0][0]())
        m = None
        for j, (_, get_vt) in enumerate(chunks):
            s = s_next
            if j + 1 < len(chunks):
                s_next = scores_t(chunks[j + 1][0]())
            m = soft_pv(s, get_vt(), m)
        acc = acc_ref[...]
        o_t = acc[:LANES] / acc[LANES:LANES + 1]
        o2 = jnp.concatenate([o_t[:HALF, :tq], o_t[HALF:, tq:]], axis=0)
        o_ref[...] = o2.T.astype(o_ref.dtype)

    ctx_chunk = (lambda: kc_ref[...], lambda: vtc_ref[...])
    lat_chunks = [
        (functools.partial(lambda j: kl_ref[j * tk:(j + 1) * tk, :], j),
         functools.partial(lambda j: vtl_ref[:, j * tk:(j + 1) * tk], j))
        for j in range(n_lat_chunks)]

    @pl.when(qt < nq_lat)
    def _():
        run([ctx_chunk] + lat_chunks)

    if ctx_queries:
        @pl.when(qt >= nq_lat)
        def _():
            run([ctx_chunk])


def _attention(q, k, vt, *, B, S, L, shared_k, ctx_queries):
    T = q.shape[0]
    tq = L
    tk = min(512, S)
    units = 8
    nq_lat = S // tq
    nq = nq_lat + (1 if ctx_queries else 0)
    ctx_blk0 = B * S // L

    def qrow(b, qt):
        return jnp.where(qt < nq_lat, b * nq_lat + qt, ctx_blk0 + b)

    if shared_k:
        q_spec = pl.BlockSpec((tq, LANES), lambda b, p, qt: (qrow(b, qt), p))
        kl_spec = pl.BlockSpec((None, S, LANES), lambda b, p, qt: (p // 2, b, 0))
        kc_spec = pl.BlockSpec((None, L, LANES), lambda b, p, qt: (p // 2, ctx_blk0 + b, 0))
        vl_spec = pl.BlockSpec((None, VT_ROWS, S), lambda b, p, qt: (p // 2, 0, b))
        vc_spec = pl.BlockSpec((None, VT_ROWS, L), lambda b, p, qt: (p // 2, 0, ctx_blk0 + b))
    else:
        q_spec = pl.BlockSpec((tq, 2 * LANES), lambda b, p, qt: (qrow(b, qt), p))
        kl_spec = pl.BlockSpec((S, 2 * LANES), lambda b, p, qt: (b, p))
        kc_spec = pl.BlockSpec((L, 2 * LANES), lambda b, p, qt: (ctx_blk0 + b, p))
        vl_spec = pl.BlockSpec((None, VT_ROWS, S), lambda b, p, qt: (p, 0, b))
        vc_spec = pl.BlockSpec((None, VT_ROWS, L), lambda b, p, qt: (p, 0, ctx_blk0 + b))
    return pl.pallas_call(
        functools.partial(_attn_kernel, shared_k=shared_k, tq=tq, tk=tk,
                          n_lat_chunks=S // tk, nq_lat=nq_lat, ctx_queries=ctx_queries),
        grid=(B, units, nq),
        in_specs=[q_spec, kl_spec, kc_spec, vl_spec, vc_spec],
        out_specs=pl.BlockSpec((tq, LANES), lambda b, p, qt: (qrow(b, qt), p)),
        out_shape=jax.ShapeDtypeStruct((T if ctx_queries else B * S, units * LANES), BF16),
        scratch_shapes=[pltpu.VMEM((VT_ROWS, 2 * tq), F32)],
        compiler_params=_cparams(3),
        name="attention",
    )(q, k, k, vt, vt)


def _linear_resid_kernel(a_ref, w_ref, x_ref, mod_ref, g_ref, o_ref, *, gate_row):
    y = _dot(a_ref[...], w_ref[...])
    o_ref[...] = x_ref[...] + mod_ref[gate_row:gate_row + 1, :] * _rms(y, g_ref[...])


def _linear_resid(A, W, X, mod, g, geom, *, gate_row, n_tiles):
    tm = geom.tm
    return pl.pallas_call(
        functools.partial(_linear_resid_kernel, gate_row=gate_row),
        grid=(n_tiles,),
        in_specs=[
            _row_spec(tm, A.shape[1]), _full_spec(W.shape), _row_spec(tm, D_MODEL),
            geom.mod_spec(), _full_spec((1, D_MODEL)),
        ],
        out_specs=_row_spec(tm, D_MODEL),
        out_shape=jax.ShapeDtypeStruct((n_tiles * tm, D_MODEL), F32),
        compiler_params=_cparams(1),
        name="linear_resid",
    )(A, W, X, mod, g)


def _norm_linear_kernel(x_ref, mod_ref, g_ref, w_ref, o_ref):
    h = _norm_mod(x_ref[...], g_ref[...], mod_ref[0:1, :], mod_ref[1:2, :]).astype(BF16)
    o_ref[...] = _dot(h, w_ref[...]).astype(o_ref.dtype)


def _norm_linear(X, mod, g, W, geom):
    tm = geom.tm
    n = W.shape[1]
    return pl.pallas_call(
        _norm_linear_kernel,
        grid=(geom.n_all,),
        in_specs=[_row_spec(tm, D_MODEL), geom.mod_spec(), _full_spec((1, D_MODEL)),
                  _full_spec(W.shape)],
        out_specs=_row_spec(tm, n),
        out_shape=jax.ShapeDtypeStruct((X.shape[0], n), BF16),
        compiler_params=_cparams(1),
        name="norm_linear",
    )(X, mod, g, W)


def _seq_dft_kernel(c_ref, s_ref, y1_ref, y2_ref, o_ref, acc_ref):
    k = pl.program_id(2)

    @pl.when(k == 0)
    def _():
        acc_ref[...] = jnp.zeros(acc_ref.shape, F32)

    acc_ref[...] += _dot(c_ref[...], y1_ref[...]) + _dot(s_ref[...], y2_ref[...])

    @pl.when(k == pl.num_programs(2) - 1)
    def _():
        o_ref[...] = acc_ref[...].astype(o_ref.dtype)


def _seq_dft(cmat, smat, Y, *, n_batch, n_seq, row_blk0):
    tu = min(1024, n_seq)
    nu = n_seq // tu
    return pl.pallas_call(
        _seq_dft_kernel,
        grid=(n_batch, nu, nu),
        in_specs=[
            pl.BlockSpec((tu, tu), lambda b, u, k: (u, k)),
            pl.BlockSpec((tu, tu), lambda b, u, k: (u, k)),
            pl.BlockSpec((tu, D_MODEL), lambda b, u, k: (row_blk0 + b * nu + k, 0)),
            pl.BlockSpec((tu, D_MODEL), lambda b, u, k: (row_blk0 + b * nu + k, 1)),
        ],
        out_specs=pl.BlockSpec((tu, D_MODEL), lambda b, u, k: (b * nu + u, 0)),
        out_shape=jax.ShapeDtypeStruct((n_batch * n_seq, D_MODEL), BF16),
        scratch_shapes=[pltpu.VMEM((tu, D_MODEL), F32)],
        compiler_params=_cparams(3),
        name="seq_dft",
    )(cmat, smat, Y, Y)


def _swiglu_chunks(h, wgu_ref, wd_ref, acc_ref, ff, tf):
    for c in range(ff // tf):
        g = _dot(h, wgu_ref[:, c * tf:(c + 1) * tf])
        u = _dot(h, wgu_ref[:, ff + c * tf:ff + (c + 1) * tf])
        a = ((g * jax.nn.sigmoid(g)) * u).astype(BF16)
        part = _dot(a, wd_ref[c * tf:(c + 1) * tf, :])
        if c == 0:
            acc_ref[...] = part
        else:
            acc_ref[...] += part


def _ffn_dense_kernel(x_ref, mod_ref, g2_ref, g3_ref, wgu_ref, wd_ref, o_ref, acc_ref,
                      *, ff, tf):
    x = x_ref[...]
    h = _norm_mod(x, g2_ref[...], mod_ref[3:4, :], mod_ref[4:5, :]).astype(BF16)
    _swiglu_chunks(h, wgu_ref, wd_ref, acc_ref, ff, tf)
    o_ref[...] = x + mod_ref[5:6, :] * _rms(acc_ref[...], g3_ref[...])


def _ffn_dense(X, mod, g2, g3, wgu, wd, geom):
    tm = geom.tm
    ff = wd.shape[0]
    single = pl.Buffered(1)
    return pl.pallas_call(
        functools.partial(_ffn_dense_kernel, ff=ff, tf=256),
        grid=(geom.n_all,),
        in_specs=[
            _row_spec(tm, D_MODEL), geom.mod_spec(), _full_spec((1, D_MODEL)),
            _full_spec((1, D_MODEL)),
            pl.BlockSpec(wgu.shape, lambda i: (0, 0), pipeline_mode=single),
            pl.BlockSpec(wd.shape, lambda i: (0, 0), pipeline_mode=single),
        ],
        out_specs=_row_spec(tm, D_MODEL),
        out_shape=jax.ShapeDtypeStruct(X.shape, F32),
        scratch_shapes=[pltpu.VMEM((tm, D_MODEL), F32)],
        compiler_params=_cparams(1),
        name="ffn_dense",
    )(X, mod, g2, g3, wgu, wd)


def _router_kernel(x_ref, mod_ref, g_ref, wr_ref, h_ref, r_ref):
    h = _norm_mod(x_ref[...], g_ref[...], mod_ref[3:4, :], mod_ref[4:5, :])
    h_ref[...] = h
    logits = jnp.dot(h, wr_ref[...], precision=lax.Precision.HIGHEST,
                     preferred_element_type=F32)
    lane = lax.broadcasted_iota(jnp.int32, logits.shape, 1)
    lane_f = lane.astype(F32)
    neg = jnp.float32(-jnp.inf)
    logits = jnp.where(lane < N_EXPERTS, logits, neg)
    v1 = jnp.max(logits, axis=-1, keepdims=True)
    i1 = jnp.min(jnp.where(logits == v1, lane_f, float(LANES)), axis=-1, keepdims=True)
    rest = jnp.where(lane_f == i1, neg, logits)
    v2 = jnp.max(rest, axis=-1, keepdims=True)
    i2 = jnp.min(jnp.where(rest == v2, lane_f, float(LANES)), axis=-1, keepdims=True)
    e2 = jnp.exp(v2 - v1)
    den = 1.0 + e2
    w1 = 1.0 / den
    w2 = e2 / den
    out = jnp.where(lane == 0, w1, 0.0)
    out = jnp.where(lane == 1, w2, out)
    out = jnp.where(lane == 2, i1, out)
    out = jnp.where(lane == 3, i2, out)
    r_ref[...] = out


def _router(X, mod, g, wr, geom, n_tiles):
    tm = geom.tm
    rows = n_tiles * tm
    return pl.pallas_call(
        _router_kernel,
        grid=(n_tiles,),
        in_specs=[_row_spec(tm, D_MODEL), geom.mod_spec(), _full_spec((1, D_MODEL)),
                  _full_spec(wr.shape)],
        out_specs=[_row_spec(tm, D_MODEL), _row_spec(tm, LANES)],
        out_shape=[jax.ShapeDtypeStruct((rows, D_MODEL), F32),
                   jax.ShapeDtypeStruct((rows, LANES), F32)],
        compiler_params=_cparams(1),
        name="moe_router",
    )(X, mod, g, wr)


def _start_row_gather(src_hbm, idx_ref, dst, sem, n_rows):
    def body(r, carry):
        t = idx_ref[0, 0, r]
        pltpu.make_async_copy(src_hbm.at[pl.ds(t, 1), :], dst.at[pl.ds(r, 1), :], sem).start()
        return carry

    lax.fori_loop(0, n_rows, body, 0)


def _wait_row_gather(src_hbm, dst, sem, n_rows):
    pltpu.make_async_copy(src_hbm.at[pl.ds(0, n_rows), :], dst, sem).wait()


def _moe_ffn_kernel(te_ref, nu_ref, idx_ref, idx_next_ref, h_hbm, wgu_ref, wd_ref, o_ref,
                    gbuf, sem, acc_ref, *, tm, ff, tf):
    i = pl.program_id(0)
    n_used = nu_ref[0]
    slot = i % 2

    @pl.when(i == 0)
    def _():
        _start_row_gather(h_hbm, idx_ref, gbuf.at[0], sem.at[0], tm)

    @pl.when(i < n_used)
    def _():
        _wait_row_gather(h_hbm, gbuf.at[slot], sem.at[slot], tm)

        @pl.when(i + 1 < n_used)
        def _():
            _start_row_gather(h_hbm, idx_next_ref, gbuf.at[1 - slot], sem.at[1 - slot], tm)

        h = gbuf[slot].astype(BF16)
        _swiglu_chunks(h, wgu_ref, wd_ref, acc_ref, ff, tf)
        o_ref[...] = acc_ref[...]

    @pl.when(i >= n_used)
    def _():
        o_ref[...] = jnp.zeros(o_ref.shape, o_ref.dtype)


def _moe_ffn(h2, wgu, wd, tile_expert, n_used, slot_token, *, tm, n_tiles):
    ff = wd.shape[1]
    single = pl.Buffered(1)
    grid_spec = pltpu.PrefetchScalarGridSpec(
        num_scalar_prefetch=2,
        grid=(n_tiles,),
        in_specs=[
            pl.BlockSpec((1, 1, tm), lambda i, te, nu: (i, 0, 0), memory_space=pltpu.SMEM),
            pl.BlockSpec((1, 1, tm), lambda i, te, nu: (jnp.minimum(i + 1, n_tiles - 1), 0, 0),
                         memory_space=pltpu.SMEM),
            pl.BlockSpec(memory_space=pl.ANY),
            pl.BlockSpec((None, D_MODEL, 2 * ff), lambda i, te, nu: (te[i], 0, 0),
                         pipeline_mode=single),
            pl.BlockSpec((None, ff, D_MODEL), lambda i, te, nu: (te[i], 0, 0),
                         pipeline_mode=single),
        ],
        out_specs=pl.BlockSpec((tm, D_MODEL), lambda i, te, nu: (i, 0)),
        scratch_shapes=[
            pltpu.VMEM((2, tm, D_MODEL), F32),
            pltpu.SemaphoreType.DMA((2,)),
            pltpu.VMEM((tm, D_MODEL), F32),
        ],
    )
    return pl.pallas_call(
        functools.partial(_moe_ffn_kernel, tm=tm, ff=ff, tf=512),
        grid_spec=grid_spec,
        out_shape=jax.ShapeDtypeStruct((n_tiles * tm, D_MODEL), F32),
        compiler_params=_cparams(1),
        name="moe_ffn",
    )(tile_expert, n_used, slot_token, slot_token, h2, wgu, wd)


def _moe_combine_kernel(idx_ref, idx_next_ref, y_hbm, r_ref, x_ref, mod_ref, g_ref, o_ref,
                        gbuf, sem, *, tm):
    i = pl.program_id(0)
    n = pl.num_programs(0)
    slot = i % 2

    @pl.when(i == 0)
    def _():
        _start_row_gather(y_hbm, idx_ref, gbuf.at[0], sem.at[0], 2 * tm)

    _wait_row_gather(y_hbm, gbuf.at[slot], sem.at[slot], 2 * tm)

    @pl.when(i + 1 < n)
    def _():
        _start_row_gather(y_hbm, idx_next_ref, gbuf.at[1 - slot], sem.at[1 - slot], 2 * tm)

    r = r_ref[...]
    y = r[:, 0:1] * gbuf[slot, 0:tm, :] + r[:, 1:2] * gbuf[slot, tm:2 * tm, :]
    o_ref[...] = x_ref[...] + mod_ref[5:6, :] * _rms(y, g_ref[...])


def _moe_combine(Y, tok_slot, route, X, mod, g, geom, n_tiles):
    tm = geom.tm
    return pl.pallas_call(
        functools.partial(_moe_combine_kernel, tm=tm),
        grid=(n_tiles,),
        in_specs=[
            pl.BlockSpec((1, 1, 2 * tm), lambda i: (i, 0, 0), memory_space=pltpu.SMEM),
            pl.BlockSpec((1, 1, 2 * tm), lambda i: (jnp.minimum(i + 1, n_tiles - 1), 0, 0),
                         memory_space=pltpu.SMEM),
            pl.BlockSpec(memory_space=pl.ANY),
            _row_spec(tm, LANES), _row_spec(tm, D_MODEL), geom.mod_spec(),
            _full_spec((1, D_MODEL)),
        ],
        out_specs=_row_spec(tm, D_MODEL),
        out_shape=jax.ShapeDtypeStruct((n_tiles * tm, D_MODEL), F32),
        scratch_shapes=[pltpu.VMEM((2, 2 * tm, D_MODEL), F32), pltpu.SemaphoreType.DMA((2,))],
        compiler_params=_cparams(1),
        name="moe_combine",
    )(tok_slot, tok_slot, Y, route, X, mod, g)


def _moe_layer(X, mod, g2, g3, w_router, wgu, wd, geom, n_tiles, tm_e):
    tm = geom.tm
    rows = n_tiles * tm
    wr = jnp.pad(w_router, ((0, 0), (0, LANES - N_EXPERTS)))
    h2, route = _router(X, mod, g2, wr, geom, n_tiles)

    e_flat = route[:, 2:4].astype(jnp.int32).reshape(-1)
    onehot = (e_flat[:, None] == jnp.arange(N_EXPERTS, dtype=jnp.int32)[None, :]).astype(jnp.int32)
    csum = jnp.cumsum(onehot, axis=0)
    rank = jnp.sum(csum * onehot, axis=1) - 1
    counts = csum[-1]
    tiles_e = (counts + tm_e - 1) // tm_e
    tile_end = jnp.cumsum(tiles_e)
    tile_start = tile_end - tiles_e
    n_used = tile_end[-1]
    slot = tile_start[e_flat] * tm_e + rank
    n_tiles_e = (2 * rows) // tm_e + N_EXPERTS
    tok = jnp.arange(2 * rows, dtype=jnp.int32) // 2
    slot_token = jnp.zeros((n_tiles_e * tm_e,), jnp.int32).at[slot].set(tok)
    tile_ids = jnp.minimum(jnp.arange(n_tiles_e, dtype=jnp.int32), n_used - 1)
    tile_expert = jnp.sum((tile_ids[:, None] >= tile_end[None, :]).astype(jnp.int32), axis=1)
    tile_expert = jnp.minimum(tile_expert, N_EXPERTS - 1).astype(jnp.int32)

    Y = _moe_ffn(h2, wgu, wd, tile_expert, n_used.reshape(1).astype(jnp.int32),
                 slot_token.reshape(n_tiles_e, 1, tm_e), tm=tm_e, n_tiles=n_tiles_e)
    slot2 = slot.reshape(rows, 2)
    tok_slot = jnp.concatenate(
        [slot2[:, 0].reshape(n_tiles, 1, tm), slot2[:, 1].reshape(n_tiles, 1, tm)], axis=2)
    return _moe_combine(Y, tok_slot.astype(jnp.int32), route, X, mod, g3, geom, n_tiles)


def _rope_tables(S, tm, rot_dim, lane0, reps):
    nf = rot_dim // 4
    t = jnp.arange(S, dtype=jnp.int32)
    rows = (t // GRID_W).astype(F32)
    cols = (t % GRID_W).astype(F32)
    inv = ROPE_BASE ** (-jnp.arange(nf, dtype=F32) / nf)
    ang_r = rows[:, None] * inv[None, :]
    ang_c = cols[:, None] * inv[None, :]
    cr, sr, cc, sc = jnp.cos(ang_r), jnp.sin(ang_r), jnp.cos(ang_c), jnp.sin(ang_c)
    cos_blk = jnp.concatenate([cr, cr, cc, cc], axis=1)
    sin_blk = jnp.concatenate([-sr, sr, -sc, sc], axis=1)
    cos = jnp.ones((S, LANES), F32)
    sin = jnp.zeros((S, LANES), F32)
    for r in range(reps):
        a = lane0 + r * rot_dim
        cos = cos.at[:, a:a + rot_dim].set(cos_blk)
        sin = sin.at[:, a:a + rot_dim].set(sin_blk)
    cos = jnp.concatenate([cos, jnp.ones((tm, LANES), F32)], axis=0)
    sin = jnp.concatenate([sin, jnp.zeros((tm, LANES), F32)], axis=0)
    return cos, sin


def _dft_mats(n, scale):
    u = jnp.arange(n, dtype=jnp.int32)
    idx = (u[:, None] * u[None, :]) % n
    ang = idx.astype(F32) * (2.0 * math.pi / n)
    return (jnp.cos(ang) * scale).astype(BF16), (-jnp.sin(ang) * scale).astype(BF16)


def _channel_dft_weight():
    n = D_MODEL // F_GROUPS
    v = jnp.arange(n, dtype=jnp.int32)
    ang = ((v[:, None] * v[None, :]) % n).astype(F32) * (2.0 * math.pi / n)
    eye = jnp.eye(F_GROUPS, dtype=F32)
    wc = jnp.kron(eye, jnp.cos(ang))
    ws = jnp.kron(eye, jnp.sin(ang))
    return jnp.concatenate([wc, ws], axis=1).astype(BF16)


def kernel(x, c, ctx, c_ctx, w_mod, b_mod, norm_g, a_w_qkv, a_g_q, a_g_k, a_w_o, f_w,
           m_w_down, m_g_cq, m_g_ckv, m_w_uq, m_w_ukv, m_w_o, d_w_gu, d_w_down,
           e_w_router, e_w_gu, e_w_down):
    B, S, D = x.shape
    L = ctx.shape[1]
    depth = w_mod.shape[0]
    assert D == D_MODEL and B + 1 <= N_MOD_ROWS and S % L == 0
    tm = min(512, S)
    geom = _Geom(B, S, L, tm)
    n_lat, n_all = geom.n_lat, geom.n_all

    X = jnp.concatenate([x.reshape(B * S, D), ctx.reshape(B * L, D)], axis=0)
    cvec = jnp.zeros((N_MOD_ROWS, D), F32).at[:B].set(c).at[B].set(c_ctx)
    mods = _mod_vectors(cvec, w_mod, b_mod)

    rope_a = _rope_tables(S, tm, A_HEAD_DIM, 0, 2)
    rope_m = _rope_tables(S, tm, M_ROPE, M_NOPE, 1)

    for i in range(depth):
        last = i == depth - 1
        kind = i % N_MIXERS
        j = i // N_MIXERS
        mod = mods[i]
        g = norm_g[i].reshape(4, 1, D)
        n_out = n_lat if last else n_all

        if kind == 0:
            gq = jnp.tile(a_g_q[j], 2).reshape(1, LANES)
            gk = jnp.tile(a_g_k[j], 2).reshape(1, LANES)
            q, kd, vd = _gqa_proj(X, mod, g[0], a_w_qkv[j].astype(BF16), gq, gk,
                                  rope_a[0], rope_a[1], geom)
            att = _attention(q, kd, vd, B=B, S=S, L=L, shared_k=True, ctx_queries=not last)
            X = _linear_resid(att, a_w_o[j].astype(BF16), X, mod, g[1], geom,
                              gate_row=2, n_tiles=n_out)
        elif kind == 1:
            Y = _norm_linear(X, mod, g[0], _channel_dft_weight(), geom)
            n_ch = D // F_GROUPS
            cl, sl = _dft_mats(S, 1.0 / math.sqrt(S * n_ch))
            Z = _seq_dft(cl, sl, Y, n_batch=B, n_seq=S, row_blk0=0)
            if not last:
                cc_, sc_ = _dft_mats(L, 1.0 / math.sqrt(L * n_ch))
                Zc = _seq_dft(cc_, sc_, Y, n_batch=B, n_seq=L, row_blk0=B * S // L)
                Z = jnp.concatenate([Z, Zc], axis=0)
            X = _linear_resid(Z, f_w[j].astype(BF16), X, mod, g[1], geom,
                              gate_row=2, n_tiles=n_out)
        else:
            wd = m_w_down[j]
            wdn = jnp.concatenate([
                wd[:, :M_Q_LORA + M_KV_LORA],
                jnp.zeros((D, M_NOPE), F32), wd[:, M_Q_LORA + M_KV_LORA:],
                jnp.zeros((D, LANES - M_NOPE - M_ROPE), F32)], axis=1).astype(BF16)
            wuq = jnp.pad(m_w_uq[j].reshape(M_Q_LORA, M_HEADS, M_NOPE + M_ROPE),
                          ((0, 0), (0, 0), (0, LANES - M_NOPE - M_ROPE)))
            wuq = wuq.reshape(M_Q_LORA, M_HEADS * LANES).astype(BF16)
            wukv = m_w_ukv[j].reshape(M_KV_LORA, M_HEADS, M_NOPE + M_V)
            wuk = jnp.pad(wukv[:, :, :M_NOPE], ((0, 0), (0, 0), (0, LANES - M_NOPE)))
            wuk = wuk.reshape(M_KV_LORA, M_HEADS * LANES).astype(BF16)
            wuv = wukv[:, :, M_NOPE:].reshape(M_KV_LORA, M_HEADS * M_V).astype(BF16)
            q, k, v = _mla_proj(X, mod, g[0], wdn, m_g_cq[j].reshape(1, -1),
                                m_g_ckv[j].reshape(1, -1), wuq, wuk, wuv,
                                rope_m[0], rope_m[1], geom)
            att = _attention(q, k, v, B=B, S=S, L=L, shared_k=False, ctx_queries=not last)
            X = _linear_resid(att, m_w_o[j].astype(BF16), X, mod, g[1], geom,
                              gate_row=2, n_tiles=n_out)

        if i % 2 == 0:
            assert not last
            X = _ffn_dense(X, mod, g[2], g[3], d_w_gu[i // 2].astype(BF16),
                           d_w_down[i // 2].astype(BF16), geom)
        else:
            X = _moe_layer(X, mod, g[2], g[3], e_w_router[i // 2],
                           e_w_gu[i // 2].astype(BF16), e_w_down[i // 2].astype(BF16),
                           geom, n_out, min(512, S))

    return X[:B * S].reshape(B, S, D)
```

```python
import functools
import math

import jax
import jax.numpy as jnp
from jax import lax
from jax.experimental import pallas as pl
from jax.experimental.pallas import tpu as pltpu

F32 = jnp.float32
BF16 = jnp.bfloat16

D_MODEL = 1024
GRID_W = 64
NORM_EPS = 1e-6
ROPE_BASE = 10000.0
LANES = 128
HALF = LANES // 2

A_HEADS = 16
A_KV_HEADS = 4
A_HEAD_DIM = 64
F_GROUPS = 4
M_HEADS = 16
M_Q_LORA = 384
M_KV_LORA = 256
M_NOPE = 64
M_ROPE = 32
M_V = 64
N_EXPERTS = 8
N_MIXERS = 3

VT_ROWS = LANES + 16
LOG2_E = math.log2(math.e)
VMEM_LIMIT = 56 * 1024 * 1024
N_MOD_ROWS = 16


def _cparams(n_axes):
    return pltpu.CompilerParams(
        dimension_semantics=("arbitrary",) * n_axes, vmem_limit_bytes=VMEM_LIMIT)


def _rms(x, g):
    y = x * lax.rsqrt(jnp.mean(x * x, axis=-1, keepdims=True) + NORM_EPS)
    return y * g


def _norm_mod(x, g, shift, scale):
    return _rms(x, g) * (1.0 + scale) + shift


def _half_rms(y, g):
    lo = lax.broadcasted_iota(jnp.int32, y.shape, 1) < HALF
    s = y * y
    s_lo = jnp.sum(jnp.where(lo, s, 0.0), axis=-1, keepdims=True)
    s_hi = jnp.sum(jnp.where(lo, 0.0, s), axis=-1, keepdims=True)
    ms = jnp.where(lo, s_lo, s_hi) * (1.0 / HALF)
    return (y * lax.rsqrt(ms + NORM_EPS)) * g


def _rope_chunk(y, cos, sin, width):
    lane = lax.broadcasted_iota(jnp.int32, y.shape, 1)
    first = ((lane // width) % 2) == 0
    partner = jnp.where(first, pltpu.roll(y, LANES - width, 1), pltpu.roll(y, width, 1))
    return y * cos + partner * sin


def _dup_halves(y):
    lo = lax.broadcasted_iota(jnp.int32, y.shape, 1) < HALF
    r = pltpu.roll(y, HALF, 1)
    return jnp.where(lo, y, r), jnp.where(lo, r, y)


def _dot(a, b):
    return jnp.dot(a, b, preferred_element_type=F32)


def _mod_kernel(c_ref, w_ref, b_ref, o_ref):
    c = c_ref[...]
    sc = c * jax.nn.sigmoid(c)
    o_ref[...] = jnp.dot(sc, w_ref[...], precision=lax.Precision.HIGHEST,
                         preferred_element_type=F32) + b_ref[...]


def _mod_vectors(cvec, w_mod, b_mod):
    depth, d, n = w_mod.shape
    tn = 1536
    out = pl.pallas_call(
        _mod_kernel,
        grid=(depth, n // tn),
        in_specs=[
            pl.BlockSpec((N_MOD_ROWS, d), lambda i, j: (0, 0)),
            pl.BlockSpec((None, d, tn), lambda i, j: (i, 0, j)),
            pl.BlockSpec((None, 1, tn), lambda i, j: (i, 0, j)),
        ],
        out_specs=pl.BlockSpec((None, N_MOD_ROWS, tn), lambda i, j: (i, 0, j)),
        out_shape=jax.ShapeDtypeStruct((depth, N_MOD_ROWS, n), F32),
        compiler_params=_cparams(2),
        name="mod_vectors",
    )(cvec, w_mod, b_mod.reshape(depth, 1, n))
    return out.reshape(depth, N_MOD_ROWS, 6, d)


class _Geom:
    def __init__(self, B, S, L, tm):
        assert S % tm == 0 and (B * L) % tm == 0
        self.B, self.S, self.L, self.tm = B, S, L, tm
        self.n_lat = B * S // tm
        self.n_all = (B * S + B * L) // tm
        self.per_batch = S // tm

    def mod_map(self, i):
        return (jnp.where(i < self.n_lat, i // self.per_batch, self.B), 0, 0)

    def table_map(self, i):
        return (jnp.where(i < self.n_lat, i % self.per_batch, self.per_batch), 0)

    def mod_spec(self):
        return pl.BlockSpec((None, 6, D_MODEL), lambda i: self.mod_map(i))


def _row_spec(tm, n):
    return pl.BlockSpec((tm, n), lambda i: (i, 0))


def _full_spec(shape):
    nd = len(shape)
    return pl.BlockSpec(shape, lambda i: (0,) * nd)


def _gqa_proj_kernel(x_ref, mod_ref, g_ref, w_ref, gq_ref, gk_ref, cos_ref, sin_ref,
                     q_ref, k_ref, vt_ref, *, scale):
    h = _norm_mod(x_ref[...], g_ref[...], mod_ref[0:1, :], mod_ref[1:2, :]).astype(BF16)
    qkv = _dot(h, w_ref[...])
    cos = cos_ref[...]
    sin = sin_ref[...]
    nq = A_HEADS * A_HEAD_DIM // LANES
    nk = A_KV_HEADS * A_HEAD_DIM // LANES
    for c in range(nq):
        y = _half_rms(qkv[:, c * LANES:(c + 1) * LANES], gq_ref[...])
        y = _rope_chunk(y, cos, sin, A_HEAD_DIM // 4) * scale
        q_ref[:, c * LANES:(c + 1) * LANES] = y.astype(BF16)
    for c in range(nk):
        y = _half_rms(qkv[:, (nq + c) * LANES:(nq + c + 1) * LANES], gk_ref[...])
        y = _rope_chunk(y, cos, sin, A_HEAD_DIM // 4)
        a, b = _dup_halves(y)
        k_ref[2 * c] = a.astype(BF16)
        k_ref[2 * c + 1] = b.astype(BF16)
    for c in range(nk):
        a, b = _dup_halves(qkv[:, (nq + nk + c) * LANES:(nq + nk + c + 1) * LANES])
        vt_ref[2 * c, :LANES, :] = a.T.astype(BF16)
        vt_ref[2 * c + 1, :LANES, :] = b.T.astype(BF16)
    vt_ref[:, LANES:, :] = jnp.ones((A_KV_HEADS, VT_ROWS - LANES, vt_ref.shape[2]), BF16)


def _gqa_proj(X, mod, g, w_qkv, gq, gk, cos, sin, geom):
    tm = geom.tm
    T = X.shape[0]
    nqkv = w_qkv.shape[1]
    kv_spec = pl.BlockSpec((A_KV_HEADS, tm, LANES), lambda i: (0, i, 0))
    tab_spec = pl.BlockSpec((tm, LANES), lambda i: geom.table_map(i))
    return pl.pallas_call(
        functools.partial(_gqa_proj_kernel, scale=A_HEAD_DIM ** -0.5 * LOG2_E),
        grid=(geom.n_all,),
        in_specs=[
            _row_spec(tm, D_MODEL), geom.mod_spec(), _full_spec((1, D_MODEL)),
            _full_spec((D_MODEL, nqkv)), _full_spec((1, LANES)), _full_spec((1, LANES)),
            tab_spec, tab_spec,
        ],
        out_specs=[_row_spec(tm, A_HEADS * A_HEAD_DIM), kv_spec,
                   pl.BlockSpec((A_KV_HEADS, VT_ROWS, tm), lambda i: (0, 0, i))],
        out_shape=[
            jax.ShapeDtypeStruct((T, A_HEADS * A_HEAD_DIM), BF16),
            jax.ShapeDtypeStruct((A_KV_HEADS, T, LANES), BF16),
            jax.ShapeDtypeStruct((A_KV_HEADS, VT_ROWS, T), BF16),
        ],
        compiler_params=_cparams(1),
        name="gqa_proj",
    )(X, mod, g, w_qkv, gq, gk, cos, sin)


def _mla_proj_kernel(x_ref, mod_ref, g_ref, wdn_ref, gcq_ref, gckv_ref, wuq_ref, wuk_ref,
                     wuv_ref, cos_ref, sin_ref, q_ref, k_ref, vt_ref, *, scale):
    h = _norm_mod(x_ref[...], g_ref[...], mod_ref[0:1, :], mod_ref[1:2, :]).astype(BF16)
    d = _dot(h, wdn_ref[...])
    cos = cos_ref[...]
    sin = sin_ref[...]
    cq = _rms(d[:, :M_Q_LORA], gcq_ref[...]).astype(BF16)
    ckv = _rms(d[:, M_Q_LORA:M_Q_LORA + M_KV_LORA], gckv_ref[...]).astype(BF16)
    kr = _rope_chunk(d[:, M_Q_LORA + M_KV_LORA:], cos, sin, M_ROPE // 4)
    q = _dot(cq, wuq_ref[...])
    kn = _dot(ckv, wuk_ref[...])
    for c in range(M_HEADS):
        sl = slice(c * LANES, (c + 1) * LANES)
        q_ref[:, sl] = (_rope_chunk(q[:, sl], cos, sin, M_ROPE // 4) * scale).astype(BF16)
        k_ref[:, sl] = (kn[:, sl] + kr).astype(BF16)
    v = _dot(ckv, wuv_ref[...])
    nv = M_HEADS * M_V
    for c in range(nv // LANES):
        sl = slice(c * LANES, (c + 1) * LANES)
        vt_ref[c, :LANES, :] = v[:, sl].T.astype(BF16)
    vt_ref[:, LANES:, :] = jnp.ones((nv // LANES, VT_ROWS - LANES, vt_ref.shape[2]), BF16)


def _mla_proj(X, mod, g, wdn, gcq, gckv, wuq, wuk, wuv, cos, sin, geom):
    tm = geom.tm
    T = X.shape[0]
    tab_spec = pl.BlockSpec((tm, LANES), lambda i: geom.table_map(i))
    nq = M_HEADS * LANES
    nv = M_HEADS * M_V
    return pl.pallas_call(
        functools.partial(_mla_proj_kernel, scale=(M_NOPE + M_ROPE) ** -0.5 * LOG2_E),
        grid=(geom.n_all,),
        in_specs=[
            _row_spec(tm, D_MODEL), geom.mod_spec(), _full_spec((1, D_MODEL)),
            _full_spec(wdn.shape), _full_spec((1, M_Q_LORA)), _full_spec((1, M_KV_LORA)),
            _full_spec(wuq.shape), _full_spec(wuk.shape), _full_spec(wuv.shape),
            tab_spec, tab_spec,
        ],
        out_specs=[_row_spec(tm, nq), _row_spec(tm, nq),
                   pl.BlockSpec((nv // LANES, VT_ROWS, tm), lambda i: (0, 0, i))],
        out_shape=[
            jax.ShapeDtypeStruct((T, nq), BF16),
            jax.ShapeDtypeStruct((T, nq), BF16),
            jax.ShapeDtypeStruct((nv // LANES, VT_ROWS, T), BF16),
        ],
        compiler_params=_cparams(1),
        name="mla_proj",
    )(X, mod, g, wdn, gcq, gckv, wuq, wuk, wuv, cos, sin)


def _attn_kernel(q_ref, kl_ref, kc_ref, vtl_ref, vtc_ref, o_ref, acc_ref,
                 *, shared_k, tq, tk, n_lat_chunks, nq_lat, ctx_queries):
    qt = pl.program_id(2)
    q = q_ref[...]
    if shared_k:
        lo = lax.broadcasted_iota(jnp.int32, q.shape, 1) < HALF
        zero = jnp.zeros_like(q)
        qs = jnp.concatenate([jnp.where(lo, q, zero), jnp.where(lo, zero, q)], axis=0)

    nt = (((1,), (1,)), ((), ()))

    def scores_t(k):
        if shared_k:
            return lax.dot_general(k, qs, nt, preferred_element_type=F32)
        return jnp.concatenate([
            lax.dot_general(k[:, :LANES], q[:, :LANES], nt, preferred_element_type=F32),
            lax.dot_general(k[:, LANES:], q[:, LANES:], nt, preferred_element_type=F32),
        ], axis=1)

    def soft_pv(s, vt, m_prev):
        m_cur = jnp.max(s, axis=0, keepdims=True)
        if m_prev is None:
            p = jnp.exp2(s - m_cur)
            acc_ref[...] = _dot(vt, p.astype(BF16))
            return m_cur
        m_new = jnp.maximum(m_prev, m_cur)
        alpha = jnp.exp2(m_prev - m_new)
        p = jnp.exp2(s - m_new)
        acc_ref[...] = alpha * acc_ref[...] + _dot(vt, p.astype(BF16))
        return m_new

    def run(chunks):
        s_next = scores_t(chunks[0][0]())
        m = None
        for j, (_, get_vt) in enumerate(chunks):
            s = s_next
            if j + 1 < len(chunks):
                s_next = scores_t(chunks[j + 1][0]())
            m = soft_pv(s, get_vt(), m)
        acc = acc_ref[...]
        o_t = acc[:LANES] / acc[LANES:LANES + 1]
        o2 = jnp.concatenate([o_t[:HALF, :tq], o_t[HALF:, tq:]], axis=0)
        o_ref[...] = o2.T.astype(o_ref.dtype)

    ctx_chunk = (lambda: kc_ref[...], lambda: vtc_ref[...])
    lat_chunks = [
        (functools.partial(lambda j: kl_ref[j * tk:(j + 1) * tk, :], j),
         functools.partial(lambda j: vtl_ref[:, j * tk:(j + 1) * tk], j))
        for j in range(n_lat_chunks)]

    @pl.when(qt < nq_lat)
    def _():
        run([ctx_chunk] + lat_chunks)

    if ctx_queries:
        @pl.when(qt >= nq_lat)
        def _():
            run([ctx_chunk])


def _attention(q, k, vt, *, B, S, L, shared_k, ctx_queries):
    T = q.shape[0]
    tq = L
    tk = min(512, S)
    units = 8
    nq_lat = S // tq
    nq = nq_lat + (1 if ctx_queries else 0)
    ctx_blk0 = B * S // L

    def qrow(b, qt):
        return jnp.where(qt < nq_lat, b * nq_lat + qt, ctx_blk0 + b)

    if shared_k:
        q_spec = pl.BlockSpec((tq, LANES), lambda b, p, qt: (qrow(b, qt), p))
        kl_spec = pl.BlockSpec((None, S, LANES), lambda b, p, qt: (p // 2, b, 0))
        kc_spec = pl.BlockSpec((None, L, LANES), lambda b, p, qt: (p // 2, ctx_blk0 + b, 0))
        vl_spec = pl.BlockSpec((None, VT_ROWS, S), lambda b, p, qt: (p // 2, 0, b))
        vc_spec = pl.BlockSpec((None, VT_ROWS, L), lambda b, p, qt: (p // 2, 0, ctx_blk0 + b))
    else:
        q_spec = pl.BlockSpec((tq, 2 * LANES), lambda b, p, qt: (qrow(b, qt), p))
        kl_spec = pl.BlockSpec((S, 2 * LANES), lambda b, p, qt: (b, p))
        kc_spec = pl.BlockSpec((L, 2 * LANES), lambda b, p, qt: (ctx_blk0 + b, p))
        vl_spec = pl.BlockSpec((None, VT_ROWS, S), lambda b, p, qt: (p, 0, b))
        vc_spec = pl.BlockSpec((None, VT_ROWS, L), lambda b, p, qt: (p, 0, ctx_blk0 + b))
    return pl.pallas_call(
        functools.partial(_attn_kernel, shared_k=shared_k, tq=tq, tk=tk,
                          n_lat_chunks=S // tk, nq_lat=nq_lat, ctx_queries=ctx_queries),
        grid=(B, units, nq),
        in_specs=[q_spec, kl_spec, kc_spec, vl_spec, vc_spec],
        out_specs=pl.BlockSpec((tq, LANES), lambda b, p, qt: (qrow(b, qt), p)),
        out_shape=jax.ShapeDtypeStruct((T if ctx_queries else B * S, units * LANES), BF16),
        scratch_shapes=[pltpu.VMEM((VT_ROWS, 2 * tq), F32)],
        compiler_params=_cparams(3),
        name="attention",
    )(q, k, k, vt, vt)


def _linear_resid_kernel(a_ref, w_ref, x_ref, mod_ref, g_ref, o_ref, *, gate_row):
    y = _dot(a_ref[...], w_ref[...])
    o_ref[...] = x_ref[...] + mod_ref[gate_row:gate_row + 1, :] * _rms(y, g_ref[...])


def _linear_resid(A, W, X, mod, g, geom, *, gate_row, n_tiles):
    tm = geom.tm
    return pl.pallas_call(
        functools.partial(_linear_resid_kernel, gate_row=gate_row),
        grid=(n_tiles,),
        in_specs=[
            _row_spec(tm, A.shape[1]), _full_spec(W.shape), _row_spec(tm, D_MODEL),
            geom.mod_spec(), _full_spec((1, D_MODEL)),
        ],
        out_specs=_row_spec(tm, D_MODEL),
        out_shape=jax.ShapeDtypeStruct((n_tiles * tm, D_MODEL), F32),
        compiler_params=_cparams(1),
        name="linear_resid",
    )(A, W, X, mod, g)


def _norm_linear_kernel(x_ref, mod_ref, g_ref, w_ref, o_ref):
    h = _norm_mod(x_ref[...], g_ref[...], mod_ref[0:1, :], mod_ref[1:2, :]).astype(BF16)
    o_ref[...] = _dot(h, w_ref[...]).astype(o_ref.dtype)


def _norm_linear(X, mod, g, W, geom):
    tm = geom.tm
    n = W.shape[1]
    return pl.pallas_call(
        _norm_linear_kernel,
        grid=(geom.n_all,),
        in_specs=[_row_spec(tm, D_MODEL), geom.mod_spec(), _full_spec((1, D_MODEL)),
                  _full_spec(W.shape)],
        out_specs=_row_spec(tm, n),
        out_shape=jax.ShapeDtypeStruct((X.shape[0], n), BF16),
        compiler_params=_cparams(1),
        name="norm_linear",
    )(X, mod, g, W)


def _seq_dft_kernel(c_ref, s_ref, y1_ref, y2_ref, o_ref, acc_ref):
    k = pl.program_id(2)

    @pl.when(k == 0)
    def _():
        acc_ref[...] = jnp.zeros(acc_ref.shape, F32)

    acc_ref[...] += _dot(c_ref[...], y1_ref[...]) + _dot(s_ref[...], y2_ref[...])

    @pl.when(k == pl.num_programs(2) - 1)
    def _():
        o_ref[...] = acc_ref[...].astype(o_ref.dtype)


def _seq_dft(cmat, smat, Y, *, n_batch, n_seq, row_blk0):
    tu = min(1024, n_seq)
    nu = n_seq // tu
    return pl.pallas_call(
        _seq_dft_kernel,
        grid=(n_batch, nu, nu),
        in_specs=[
            pl.BlockSpec((tu, tu), lambda b, u, k: (u, k)),
            pl.BlockSpec((tu, tu), lambda b, u, k: (u, k)),
            pl.BlockSpec((tu, D_MODEL), lambda b, u, k: (row_blk0 + b * nu + k, 0)),
            pl.BlockSpec((tu, D_MODEL), lambda b, u, k: (row_blk0 + b * nu + k, 1)),
        ],
        out_specs=pl.BlockSpec((tu, D_MODEL), lambda b, u, k: (b * nu + u, 0)),
        out_shape=jax.ShapeDtypeStruct((n_batch * n_seq, D_MODEL), BF16),
        scratch_shapes=[pltpu.VMEM((tu, D_MODEL), F32)],
        compiler_params=_cparams(3),
        name="seq_dft",
    )(cmat, smat, Y, Y)


def _swiglu_chunks(h, wgu_ref, wd_ref, acc_ref, ff, tf, between=None):
    n = ff // tf
    for c in range(n):
        g = _dot(h, wgu_ref[:, c * tf:(c + 1) * tf])
        u = _dot(h, wgu_ref[:, ff + c * tf:ff + (c + 1) * tf])
        a = ((g * jax.nn.sigmoid(g)) * u).astype(BF16)
        part = _dot(a, wd_ref[c * tf:(c + 1) * tf, :])
        if c == 0:
            acc_ref[...] = part
        else:
            acc_ref[...] += part
        if between is not None:
            between(c, n)


def _ffn_dense_kernel(x_ref, mod_ref, g2_ref, g3_ref, wgu_ref, wd_ref, o_ref, acc_ref,
                      *, ff, tf):
    x = x_ref[...]
    h = _norm_mod(x, g2_ref[...], mod_ref[3:4, :], mod_ref[4:5, :]).astype(BF16)
    _swiglu_chunks(h, wgu_ref, wd_ref, acc_ref, ff, tf)
    o_ref[...] = x + mod_ref[5:6, :] * _rms(acc_ref[...], g3_ref[...])


def _ffn_dense(X, mod, g2, g3, wgu, wd, geom):
    tm = geom.tm
    ff = wd.shape[0]
    single = pl.Buffered(1)
    return pl.pallas_call(
        functools.partial(_ffn_dense_kernel, ff=ff, tf=256),
        grid=(geom.n_all,),
        in_specs=[
            _row_spec(tm, D_MODEL), geom.mod_spec(), _full_spec((1, D_MODEL)),
            _full_spec((1, D_MODEL)),
            pl.BlockSpec(wgu.shape, lambda i: (0, 0), pipeline_mode=single),
            pl.BlockSpec(wd.shape, lambda i: (0, 0), pipeline_mode=single),
        ],
        out_specs=_row_spec(tm, D_MODEL),
        out_shape=jax.ShapeDtypeStruct(X.shape, F32),
        scratch_shapes=[pltpu.VMEM((tm, D_MODEL), F32)],
        compiler_params=_cparams(1),
        name="ffn_dense",
    )(X, mod, g2, g3, wgu, wd)


def _router_kernel(x_ref, mod_ref, g_ref, wr_ref, h_ref, r_ref):
    h = _norm_mod(x_ref[...], g_ref[...], mod_ref[3:4, :], mod_ref[4:5, :])
    h_ref[...] = h
    logits = jnp.dot(h, wr_ref[...], precision=lax.Precision.HIGHEST,
                     preferred_element_type=F32)
    lane = lax.broadcasted_iota(jnp.int32, logits.shape, 1)
    lane_f = lane.astype(F32)
    neg = jnp.float32(-jnp.inf)
    logits = jnp.where(lane < N_EXPERTS, logits, neg)
    v1 = jnp.max(logits, axis=-1, keepdims=True)
    i1 = jnp.min(jnp.where(logits == v1, lane_f, float(LANES)), axis=-1, keepdims=True)
    rest = jnp.where(lane_f == i1, neg, logits)
    v2 = jnp.max(rest, axis=-1, keepdims=True)
    i2 = jnp.min(jnp.where(rest == v2, lane_f, float(LANES)), axis=-1, keepdims=True)
    e2 = jnp.exp(v2 - v1)
    den = 1.0 + e2
    w1 = 1.0 / den
    w2 = e2 / den
    out = jnp.where(lane == 0, w1, 0.0)
    out = jnp.where(lane == 1, w2, out)
    out = jnp.where(lane == 2, i1, out)
    out = jnp.where(lane == 3, i2, out)
    r_ref[...] = out


def _router(X, mod, g, wr, geom, n_tiles):
    tm = geom.tm
    rows = n_tiles * tm
    return pl.pallas_call(
        _router_kernel,
        grid=(n_tiles,),
        in_specs=[_row_spec(tm, D_MODEL), geom.mod_spec(), _full_spec((1, D_MODEL)),
                  _full_spec(wr.shape)],
        out_specs=[_row_spec(tm, D_MODEL), _row_spec(tm, LANES)],
        out_shape=[jax.ShapeDtypeStruct((rows, D_MODEL), F32),
                   jax.ShapeDtypeStruct((rows, LANES), F32)],
        compiler_params=_cparams(1),
        name="moe_router",
    )(X, mod, g, wr)


GATHER_UNROLL = 8


def _start_row(src_hbm, idx_ref, dst, sem, r, priority):
    t = idx_ref[0, 0, r]
    pltpu.make_async_copy(src_hbm.at[pl.ds(t, 1), :], dst.at[pl.ds(r, 1), :], sem).start(
        priority=priority)


def _start_row_gather(src_hbm, idx_ref, dst, sem, n_rows):
    def body(g, carry):
        for u in range(GATHER_UNROLL):
            _start_row(src_hbm, idx_ref, dst, sem, g * GATHER_UNROLL + u, u % 2)
        return carry

    lax.fori_loop(0, n_rows // GATHER_UNROLL, body, 0)


def _start_rows_static(src_hbm, idx_ref, dst, sem, r0, r1):
    for r in range(r0, r1):
        _start_row(src_hbm, idx_ref, dst, sem, r, r % 2)


def _wait_row_gather(src_hbm, dst, sem, n_rows):
    pltpu.make_async_copy(src_hbm.at[pl.ds(0, n_rows), :], dst, sem).wait()


def _moe_ffn_kernel(te_ref, nu_ref, idx_ref, idx_next_ref, h_hbm, wgu_ref, wd_ref, o_ref,
                    gbuf, sem, acc_ref, *, tm, ff, tf):
    i = pl.program_id(0)
    n_used = nu_ref[0]
    slot = i % 2

    @pl.when(i == 0)
    def _():
        _start_row_gather(h_hbm, idx_ref, gbuf.at[0], sem.at[0], tm)

    _wait_row_gather(h_hbm, gbuf.at[slot], sem.at[slot], tm)

    @pl.when(i < n_used)
    def _():
        h = gbuf[slot].astype(BF16)

        def issue_next(c, n):
            _start_rows_static(h_hbm, idx_next_ref, gbuf.at[1 - slot], sem.at[1 - slot],
                               c * tm // n, (c + 1) * tm // n)

        _swiglu_chunks(h, wgu_ref, wd_ref, acc_ref, ff, tf, issue_next)
        o_ref[...] = acc_ref[...]

    @pl.when(i >= n_used)
    def _():
        _start_row_gather(h_hbm, idx_next_ref, gbuf.at[1 - slot], sem.at[1 - slot], tm)
        o_ref[...] = jnp.zeros(o_ref.shape, o_ref.dtype)

    @pl.when(i == pl.num_programs(0) - 1)
    def _():
        _wait_row_gather(h_hbm, gbuf.at[1 - slot], sem.at[1 - slot], tm)


def _moe_ffn(h2, wgu, wd, tile_expert, n_used, slot_token, *, tm, n_tiles):
    ff = wd.shape[1]
    single = pl.Buffered(1)
    grid_spec = pltpu.PrefetchScalarGridSpec(
        num_scalar_prefetch=2,
        grid=(n_tiles,),
        in_specs=[
            pl.BlockSpec((1, 1, tm), lambda i, te, nu: (i, 0, 0), memory_space=pltpu.SMEM),
            pl.BlockSpec((1, 1, tm), lambda i, te, nu: (jnp.minimum(i + 1, n_tiles - 1), 0, 0),
                         memory_space=pltpu.SMEM),
            pl.BlockSpec(memory_space=pl.ANY),
            pl.BlockSpec((None, D_MODEL, 2 * ff), lambda i, te, nu: (te[i], 0, 0),
                         pipeline_mode=single),
            pl.BlockSpec((None, ff, D_MODEL), lambda i, te, nu: (te[i], 0, 0),
                         pipeline_mode=single),
        ],
        out_specs=pl.BlockSpec((tm, D_MODEL), lambda i, te, nu: (i, 0)),
        scratch_shapes=[
            pltpu.VMEM((2, tm, D_MODEL), F32),
            pltpu.SemaphoreType.DMA((2,)),
            pltpu.VMEM((tm, D_MODEL), F32),
        ],
    )
    return pl.pallas_call(
        functools.partial(_moe_ffn_kernel, tm=tm, ff=ff, tf=512),
        grid_spec=grid_spec,
        out_shape=jax.ShapeDtypeStruct((n_tiles * tm, D_MODEL), F32),
        compiler_params=_cparams(1),
        name="moe_ffn",
    )(tile_expert, n_used, slot_token, slot_token, h2, wgu, wd)


def _moe_combine_kernel(idx_ref, idx_next_ref, y_hbm, r_ref, x_ref, mod_ref, g_ref, o_ref,
                        gbuf, sem, *, tm):
    i = pl.program_id(0)
    n = pl.num_programs(0)
    slot = i % 2

    @pl.when(i == 0)
    def _():
        _start_row_gather(y_hbm, idx_ref, gbuf.at[0], sem.at[0], 2 * tm)

    _wait_row_gather(y_hbm, gbuf.at[slot], sem.at[slot], 2 * tm)

    @pl.when(i + 1 < n)
    def _():
        _start_row_gather(y_hbm, idx_next_ref, gbuf.at[1 - slot], sem.at[1 - slot], 2 * tm)

    r = r_ref[...]
    y = r[:, 0:1] * gbuf[slot, 0:tm, :] + r[:, 1:2] * gbuf[slot, tm:2 * tm, :]
    o_ref[...] = x_ref[...] + mod_ref[5:6, :] * _rms(y, g_ref[...])


def _moe_combine(Y, tok_slot, route, X, mod, g, geom, n_tiles):
    tm = geom.tm
    return pl.pallas_call(
        functools.partial(_moe_combine_kernel, tm=tm),
        grid=(n_tiles,),
        in_specs=[
            pl.BlockSpec((1, 1, 2 * tm), lambda i: (i, 0, 0), memory_space=pltpu.SMEM),
            pl.BlockSpec((1, 1, 2 * tm), lambda i: (jnp.minimum(i + 1, n_tiles - 1), 0, 0),
                         memory_space=pltpu.SMEM),
            pl.BlockSpec(memory_space=pl.ANY),
            _row_spec(tm, LANES), _row_spec(tm, D_MODEL), geom.mod_spec(),
            _full_spec((1, D_MODEL)),
        ],
        out_specs=_row_spec(tm, D_MODEL),
        out_shape=jax.ShapeDtypeStruct((n_tiles * tm, D_MODEL), F32),
        scratch_shapes=[pltpu.VMEM((2, 2 * tm, D_MODEL), F32), pltpu.SemaphoreType.DMA((2,))],
        compiler_params=_cparams(1),
        name="moe_combine",
    )(tok_slot, tok_slot, Y, route, X, mod, g)


def _moe_layer(X, mod, g2, g3, w_router, wgu, wd, geom, n_tiles, tm_e):
    tm = geom.tm
    rows = n_tiles * tm
    wr = jnp.pad(w_router, ((0, 0), (0, LANES - N_EXPERTS)))
    h2, route = _router(X, mod, g2, wr, geom, n_tiles)

    e_flat = route[:, 2:4].astype(jnp.int32).reshape(-1)
    onehot = (e_flat[:, None] == jnp.arange(N_EXPERTS, dtype=jnp.int32)[None, :]).astype(jnp.int32)
    csum = jnp.cumsum(onehot, axis=0)
    rank = jnp.sum(csum * onehot, axis=1) - 1
    counts = csum[-1]
    tiles_e = (counts + tm_e - 1) // tm_e
    tile_end = jnp.cumsum(tiles_e)
    tile_start = tile_end - tiles_e
    n_used = tile_end[-1]
    slot = tile_start[e_flat] * tm_e + rank
    n_tiles_e = (2 * rows) // tm_e + N_EXPERTS
    tok = jnp.arange(2 * rows, dtype=jnp.int32) // 2
    slot_token = jnp.zeros((n_tiles_e * tm_e,), jnp.int32).at[slot].set(tok)
    tile_ids = jnp.minimum(jnp.arange(n_tiles_e, dtype=jnp.int32), n_used - 1)
    tile_expert = jnp.sum((tile_ids[:, None] >= tile_end[None, :]).astype(jnp.int32), axis=1)
    tile_expert = jnp.minimum(tile_expert, N_EXPERTS - 1).astype(jnp.int32)

    Y = _moe_ffn(h2, wgu, wd, tile_expert, n_used.reshape(1).astype(jnp.int32),
                 slot_token.reshape(n_tiles_e, 1, tm_e), tm=tm_e, n_tiles=n_tiles_e)
    slot2 = slot.reshape(rows, 2)
    tok_slot = jnp.concatenate(
        [slot2[:, 0].reshape(n_tiles, 1, tm), slot2[:, 1].reshape(n_tiles, 1, tm)], axis=2)
    return _moe_combine(Y, tok_slot.astype(jnp.int32), route, X, mod, g3, geom, n_tiles)


def _rope_tables(S, tm, rot_dim, lane0, reps):
    nf = rot_dim // 4
    t = jnp.arange(S, dtype=jnp.int32)
    rows = (t // GRID_W).astype(F32)
    cols = (t % GRID_W).astype(F32)
    inv = ROPE_BASE ** (-jnp.arange(nf, dtype=F32) / nf)
    ang_r = rows[:, None] * inv[None, :]
    ang_c = cols[:, None] * inv[None, :]
    cr, sr, cc, sc = jnp.cos(ang_r), jnp.sin(ang_r), jnp.cos(ang_c), jnp.sin(ang_c)
    cos_blk = jnp.concatenate([cr, cr, cc, cc], axis=1)
    sin_blk = jnp.concatenate([-sr, sr, -sc, sc], axis=1)
    cos = jnp.ones((S, LANES), F32)
    sin = jnp.zeros((S, LANES), F32)
    for r in range(reps):
        a = lane0 + r * rot_dim
        cos = cos.at[:, a:a + rot_dim].set(cos_blk)
        sin = sin.at[:, a:a + rot_dim].set(sin_blk)
    cos = jnp.concatenate([cos, jnp.ones((tm, LANES), F32)], axis=0)
    sin = jnp.concatenate([sin, jnp.zeros((tm, LANES), F32)], axis=0)
    return cos, sin


def _dft_mats(n, scale):
    u = jnp.arange(n, dtype=jnp.int32)
    idx = (u[:, None] * u[None, :]) % n
    ang = idx.astype(F32) * (2.0 * math.pi / n)
    return (jnp.cos(ang) * scale).astype(BF16), (-jnp.sin(ang) * scale).astype(BF16)


def _channel_dft_weight():
    n = D_MODEL // F_GROUPS
    v = jnp.arange(n, dtype=jnp.int32)
    ang = ((v[:, None] * v[None, :]) % n).astype(F32) * (2.0 * math.pi / n)
    eye = jnp.eye(F_GROUPS, dtype=F32)
    wc = jnp.kron(eye, jnp.cos(ang))
    ws = jnp.kron(eye, jnp.sin(ang))
    return jnp.concatenate([wc, ws], axis=1).astype(BF16)


def kernel(x, c, ctx, c_ctx, w_mod, b_mod, norm_g, a_w_qkv, a_g_q, a_g_k, a_w_o, f_w,
           m_w_down, m_g_cq, m_g_ckv, m_w_uq, m_w_ukv, m_w_o, d_w_gu, d_w_down,
           e_w_router, e_w_gu, e_w_down):
    B, S, D = x.shape
    L = ctx.shape[1]
    depth = w_mod.shape[0]
    assert D == D_MODEL and B + 1 <= N_MOD_ROWS and S % L == 0
    tm = min(512, S)
    geom = _Geom(B, S, L, tm)
    n_lat, n_all = geom.n_lat, geom.n_all

    X = jnp.concatenate([x.reshape(B * S, D), ctx.reshape(B * L, D)], axis=0)
    cvec = jnp.zeros((N_MOD_ROWS, D), F32).at[:B].set(c).at[B].set(c_ctx)
    mods = _mod_vectors(cvec, w_mod, b_mod)

    rope_a = _rope_tables(S, tm, A_HEAD_DIM, 0, 2)
    rope_m = _rope_tables(S, tm, M_ROPE, M_NOPE, 1)

    for i in range(depth):
        last = i == depth - 1
        kind = i % N_MIXERS
        j = i // N_MIXERS
        mod = mods[i]
        g = norm_g[i].reshape(4, 1, D)
        n_out = n_lat if last else n_all

        if kind == 0:
            gq = jnp.tile(a_g_q[j], 2).reshape(1, LANES)
            gk = jnp.tile(a_g_k[j], 2).reshape(1, LANES)
            q, kd, vd = _gqa_proj(X, mod, g[0], a_w_qkv[j].astype(BF16), gq, gk,
                                  rope_a[0], rope_a[1], geom)
            att = _attention(q, kd, vd, B=B, S=S, L=L, shared_k=True, ctx_queries=not last)
            X = _linear_resid(att, a_w_o[j].astype(BF16), X, mod, g[1], geom,
                              gate_row=2, n_tiles=n_out)
        elif kind == 1:
            Y = _norm_linear(X, mod, g[0], _channel_dft_weight(), geom)
            n_ch = D // F_GROUPS
            cl, sl = _dft_mats(S, 1.0 / math.sqrt(S * n_ch))
            Z = _seq_dft(cl, sl, Y, n_batch=B, n_seq=S, row_blk0=0)
            if not last:
                cc_, sc_ = _dft_mats(L, 1.0 / math.sqrt(L * n_ch))
                Zc = _seq_dft(cc_, sc_, Y, n_batch=B, n_seq=L, row_blk0=B * S // L)
                Z = jnp.concatenate([Z, Zc], axis=0)
            X = _linear_resid(Z, f_w[j].astype(BF16), X, mod, g[1], geom,
                              gate_row=2, n_tiles=n_out)
        else:
            wd = m_w_down[j]
            wdn = jnp.concatenate([
                wd[:, :M_Q_LORA + M_KV_LORA],
                jnp.zeros((D, M_NOPE), F32), wd[:, M_Q_LORA + M_KV_LORA:],
                jnp.zeros((D, LANES - M_NOPE - M_ROPE), F32)], axis=1).astype(BF16)
            wuq = jnp.pad(m_w_uq[j].reshape(M_Q_LORA, M_HEADS, M_NOPE + M_ROPE),
                          ((0, 0), (0, 0), (0, LANES - M_NOPE - M_ROPE)))
            wuq = wuq.reshape(M_Q_LORA, M_HEADS * LANES).astype(BF16)
            wukv = m_w_ukv[j].reshape(M_KV_LORA, M_HEADS, M_NOPE + M_V)
            wuk = jnp.pad(wukv[:, :, :M_NOPE], ((0, 0), (0, 0), (0, LANES - M_NOPE)))
            wuk = wuk.reshape(M_KV_LORA, M_HEADS * LANES).astype(BF16)
            wuv = wukv[:, :, M_NOPE:].reshape(M_KV_LORA, M_HEADS * M_V).astype(BF16)
            q, k, v = _mla_proj(X, mod, g[0], wdn, m_g_cq[j].reshape(1, -1),
                                m_g_ckv[j].reshape(1, -1), wuq, wuk, wuv,
                                rope_m[0], rope_m[1], geom)
            att = _attention(q, k, v, B=B, S=S, L=L, shared_k=False, ctx_queries=not last)
            X = _linear_resid(att, m_w_o[j].astype(BF16), X, mod, g[1], geom,
                              gate_row=2, n_tiles=n_out)

        if i % 2 == 0:
            assert not last
            X = _ffn_dense(X, mod, g[2], g[3], d_w_gu[i // 2].astype(BF16),
                           d_w_down[i // 2].astype(BF16), geom)
        else:
            X = _moe_layer(X, mod, g[2], g[3], e_w_router[i // 2],
                           e_w_gu[i // 2].astype(BF16), e_w_down[i // 2].astype(BF16),
                           geom, n_out, min(512, S))

    return X[:B * S].reshape(B, S, D)
```

```python
import functools
import math

import jax
import jax.numpy as jnp
from jax import lax
from jax.experimental import pallas as pl
from jax.experimental.pallas import tpu as pltpu

F32 = jnp.float32
BF16 = jnp.bfloat16

D_MODEL = 1024
GRID_W = 64
NORM_EPS = 1e-6
ROPE_BASE = 10000.0
LANES = 128
HALF = LANES // 2

A_HEADS = 16
A_KV_HEADS = 4
A_HEAD_DIM = 64
F_GROUPS = 4
M_HEADS = 16
M_Q_LORA = 384
M_KV_LORA = 256
M_NOPE = 64
M_ROPE = 32
M_V = 64
N_EXPERTS = 8
N_MIXERS = 3

VT_ROWS = LANES + 16
LOG2_E = math.log2(math.e)
VMEM_LIMIT = 56 * 1024 * 1024
N_MOD_ROWS = 16


def _cparams(n_axes):
    return pltpu.CompilerParams(
        dimension_semantics=("arbitrary",) * n_axes, vmem_limit_bytes=VMEM_LIMIT)


def _rms(x, g):
    y = x * lax.rsqrt(jnp.mean(x * x, axis=-1, keepdims=True) + NORM_EPS)
    return y * g


def _norm_mod(x, g, shift, scale):
    return _rms(x, g) * (1.0 + scale) + shift


def _half_rms(y, g):
    lo = lax.broadcasted_iota(jnp.int32, y.shape, 1) < HALF
    s = y * y
    s_lo = jnp.sum(jnp.where(lo, s, 0.0), axis=-1, keepdims=True)
    s_hi = jnp.sum(jnp.where(lo, 0.0, s), axis=-1, keepdims=True)
    ms = jnp.where(lo, s_lo, s_hi) * (1.0 / HALF)
    return (y * lax.rsqrt(ms + NORM_EPS)) * g


def _rope_chunk(y, cos, sin, width):
    lane = lax.broadcasted_iota(jnp.int32, y.shape, 1)
    first = ((lane // width) % 2) == 0
    partner = jnp.where(first, pltpu.roll(y, LANES - width, 1), pltpu.roll(y, width, 1))
    return y * cos + partner * sin


def _dup_halves(y):
    lo = lax.broadcasted_iota(jnp.int32, y.shape, 1) < HALF
    r = pltpu.roll(y, HALF, 1)
    return jnp.where(lo, y, r), jnp.where(lo, r, y)


def _dot(a, b):
    return jnp.dot(a, b, preferred_element_type=F32)


def _mod_kernel(c_ref, w_ref, b_ref, o_ref):
    c = c_ref[...]
    sc = c * jax.nn.sigmoid(c)
    o_ref[...] = jnp.dot(sc, w_ref[...], precision=lax.Precision.HIGHEST,
                         preferred_element_type=F32) + b_ref[...]


def _mod_vectors(cvec, w_mod, b_mod):
    depth, d, n = w_mod.shape
    tn = 1536
    out = pl.pallas_call(
        _mod_kernel,
        grid=(depth, n // tn),
        in_specs=[
            pl.BlockSpec((N_MOD_ROWS, d), lambda i, j: (0, 0)),
            pl.BlockSpec((None, d, tn), lambda i, j: (i, 0, j)),
            pl.BlockSpec((None, 1, tn), lambda i, j: (i, 0, j)),
        ],
        out_specs=pl.BlockSpec((None, N_MOD_ROWS, tn), lambda i, j: (i, 0, j)),
        out_shape=jax.ShapeDtypeStruct((depth, N_MOD_ROWS, n), F32),
        compiler_params=_cparams(2),
        name="mod_vectors",
    )(cvec, w_mod, b_mod.reshape(depth, 1, n))
    return out.reshape(depth, N_MOD_ROWS, 6, d)


class _Geom:
    def __init__(self, B, S, L, tm):
        assert S % tm == 0 and (B * L) % tm == 0
        self.B, self.S, self.L, self.tm = B, S, L, tm
        self.n_lat = B * S // tm
        self.n_all = (B * S + B * L) // tm
        self.per_batch = S // tm

    def mod_map(self, i):
        return (jnp.where(i < self.n_lat, i // self.per_batch, self.B), 0, 0)

    def table_map(self, i):
        return (jnp.where(i < self.n_lat, i % self.per_batch, self.per_batch), 0)

    def mod_spec(self):
        return pl.BlockSpec((None, 6, D_MODEL), lambda i: self.mod_map(i))


def _row_spec(tm, n):
    return pl.BlockSpec((tm, n), lambda i: (i, 0))


def _full_spec(shape):
    nd = len(shape)
    return pl.BlockSpec(shape, lambda i: (0,) * nd)


def _gqa_proj_kernel(x_ref, mod_ref, g_ref, w_ref, gq_ref, gk_ref, cos_ref, sin_ref,
                     q_ref, k_ref, vt_ref, *, scale):
    h = _norm_mod(x_ref[...], g_ref[...], mod_ref[0:1, :], mod_ref[1:2, :]).astype(BF16)
    qkv = _dot(h, w_ref[...])
    cos = cos_ref[...]
    sin = sin_ref[...]
    nq = A_HEADS * A_HEAD_DIM // LANES
    nk = A_KV_HEADS * A_HEAD_DIM // LANES
    for c in range(nq):
        y = _half_rms(qkv[:, c * LANES:(c + 1) * LANES], gq_ref[...])
        y = _rope_chunk(y, cos, sin, A_HEAD_DIM // 4) * scale
        q_ref[:, c * LANES:(c + 1) * LANES] = y.astype(BF16)
    for c in range(nk):
        y = _half_rms(qkv[:, (nq + c) * LANES:(nq + c + 1) * LANES], gk_ref[...])
        y = _rope_chunk(y, cos, sin, A_HEAD_DIM // 4)
        a, b = _dup_halves(y)
        k_ref[2 * c] = a.astype(BF16)
        k_ref[2 * c + 1] = b.astype(BF16)
    for c in range(nk):
        a, b = _dup_halves(qkv[:, (nq + nk + c) * LANES:(nq + nk + c + 1) * LANES])
        vt_ref[2 * c, :LANES, :] = a.T.astype(BF16)
        vt_ref[2 * c + 1, :LANES, :] = b.T.astype(BF16)
    vt_ref[:, LANES:, :] = jnp.ones((A_KV_HEADS, VT_ROWS - LANES, vt_ref.shape[2]), BF16)


def _gqa_proj(X, mod, g, w_qkv, gq, gk, cos, sin, geom):
    tm = geom.tm
    T = X.shape[0]
    nqkv = w_qkv.shape[1]
    kv_spec = pl.BlockSpec((A_KV_HEADS, tm, LANES), lambda i: (0, i, 0))
    tab_spec = pl.BlockSpec((tm, LANES), lambda i: geom.table_map(i))
    return pl.pallas_call(
        functools.partial(_gqa_proj_kernel, scale=A_HEAD_DIM ** -0.5 * LOG2_E),
        grid=(geom.n_all,),
        in_specs=[
            _row_spec(tm, D_MODEL), geom.mod_spec(), _full_spec((1, D_MODEL)),
            _full_spec((D_MODEL, nqkv)), _full_spec((1, LANES)), _full_spec((1, LANES)),
            tab_spec, tab_spec,
        ],
        out_specs=[_row_spec(tm, A_HEADS * A_HEAD_DIM), kv_spec,
                   pl.BlockSpec((A_KV_HEADS, VT_ROWS, tm), lambda i: (0, 0, i))],
        out_shape=[
            jax.ShapeDtypeStruct((T, A_HEADS * A_HEAD_DIM), BF16),
            jax.ShapeDtypeStruct((A_KV_HEADS, T, LANES), BF16),
            jax.ShapeDtypeStruct((A_KV_HEADS, VT_ROWS, T), BF16),
        ],
        compiler_params=_cparams(1),
        name="gqa_proj",
    )(X, mod, g, w_qkv, gq, gk, cos, sin)


def _mla_proj_kernel(x_ref, mod_ref, g_ref, wdn_ref, gcq_ref, gckv_ref, wuq_ref, wuk_ref,
                     wuv_ref, cos_ref, sin_ref, q_ref, k_ref, vt_ref, *, scale):
    h = _norm_mod(x_ref[...], g_ref[...], mod_ref[0:1, :], mod_ref[1:2, :]).astype(BF16)
    d = _dot(h, wdn_ref[...])
    cos = cos_ref[...]
    sin = sin_ref[...]
    cq = _rms(d[:, :M_Q_LORA], gcq_ref[...]).astype(BF16)
    ckv = _rms(d[:, M_Q_LORA:M_Q_LORA + M_KV_LORA], gckv_ref[...]).astype(BF16)
    kr = _rope_chunk(d[:, M_Q_LORA + M_KV_LORA:], cos, sin, M_ROPE // 4)
    q = _dot(cq, wuq_ref[...])
    kn = _dot(ckv, wuk_ref[...])
    for c in range(M_HEADS):
        sl = slice(c * LANES, (c + 1) * LANES)
        q_ref[:, sl] = (_rope_chunk(q[:, sl], cos, sin, M_ROPE // 4) * scale).astype(BF16)
        k_ref[:, sl] = (kn[:, sl] + kr).astype(BF16)
    v = _dot(ckv, wuv_ref[...])
    nv = M_HEADS * M_V
    for c in range(nv // LANES):
        sl = slice(c * LANES, (c + 1) * LANES)
        vt_ref[c, :LANES, :] = v[:, sl].T.astype(BF16)
    vt_ref[:, LANES:, :] = jnp.ones((nv // LANES, VT_ROWS - LANES, vt_ref.shape[2]), BF16)


def _mla_proj(X, mod, g, wdn, gcq, gckv, wuq, wuk, wuv, cos, sin, geom):
    tm = geom.tm
    T = X.shape[0]
    tab_spec = pl.BlockSpec((tm, LANES), lambda i: geom.table_map(i))
    nq = M_HEADS * LANES
    nv = M_HEADS * M_V
    return pl.pallas_call(
        functools.partial(_mla_proj_kernel, scale=(M_NOPE + M_ROPE) ** -0.5 * LOG2_E),
        grid=(geom.n_all,),
        in_specs=[
            _row_spec(tm, D_MODEL), geom.mod_spec(), _full_spec((1, D_MODEL)),
            _full_spec(wdn.shape), _full_spec((1, M_Q_LORA)), _full_spec((1, M_KV_LORA)),
            _full_spec(wuq.shape), _full_spec(wuk.shape), _full_spec(wuv.shape),
            tab_spec, tab_spec,
        ],
        out_specs=[_row_spec(tm, nq), _row_spec(tm, nq),
                   pl.BlockSpec((nv // LANES, VT_ROWS, tm), lambda i: (0, 0, i))],
        out_shape=[
            jax.ShapeDtypeStruct((T, nq), BF16),
            jax.ShapeDtypeStruct((T, nq), BF16),
            jax.ShapeDtypeStruct((nv // LANES, VT_ROWS, T), BF16),
        ],
        compiler_params=_cparams(1),
        name="mla_proj",
    )(X, mod, g, wdn, gcq, gckv, wuq, wuk, wuv, cos, sin)


def _attn_kernel(q_ref, kl_ref, kc_ref, vtl_ref, vtc_ref, o_ref, acc_ref,
                 *, shared_k, tq, tk, n_lat_chunks, nq_lat, ctx_queries):
    qt = pl.program_id(2)
    q = q_ref[...]
    if shared_k:
        lo = lax.broadcasted_iota(jnp.int32, (tq, LANES), 1) < HALF
        zero = jnp.zeros((tq, LANES), q.dtype)
        pieces = []
        for c in range(q.shape[1] // LANES):
            qc = q[:, c * LANES:(c + 1) * LANES]
            pieces += [jnp.where(lo, qc, zero), jnp.where(lo, zero, qc)]
        qs = jnp.concatenate(pieces, axis=0)

    nt = (((1,), (1,)), ((), ()))

    def scores_t(k):
        if shared_k:
            return lax.dot_general(k, qs, nt, preferred_element_type=F32)
        return jnp.concatenate([
            lax.dot_general(k[:, :LANES], q[:, :LANES], nt, preferred_element_type=F32),
            lax.dot_general(k[:, LANES:], q[:, LANES:], nt, preferred_element_type=F32),
        ], axis=1)

    def soft_pv(s, vt, m_prev):
        m_cur = jnp.max(s, axis=0, keepdims=True)
        if m_prev is None:
            p = jnp.exp2(s - m_cur)
            acc_ref[...] = _dot(vt, p.astype(BF16))
            return m_cur
        m_new = jnp.maximum(m_prev, m_cur)
        alpha = jnp.exp2(m_prev - m_new)
        p = jnp.exp2(s - m_new)
        acc_ref[...] = alpha * acc_ref[...] + _dot(vt, p.astype(BF16))
        return m_new

    def run(chunks):
        s_next = scores_t(chunks[0][0]())
        m = None
        for j, (_, get_vt) in enumerate(chunks):
            s = s_next
            if j + 1 < len(chunks):
                s_next = scores_t(chunks[j + 1][0]())
            m = soft_pv(s, get_vt(), m)
        acc = acc_ref[...]
        o_t = acc[:LANES] * (1.0 / acc[LANES:LANES + 1])
        for c in range(acc.shape[1] // (2 * tq)):
            a = o_t[:HALF, 2 * c * tq:(2 * c + 1) * tq]
            b = o_t[HALF:, (2 * c + 1) * tq:(2 * c + 2) * tq]
            o2 = jnp.concatenate([a, b], axis=0)
            o_ref[:, c * LANES:(c + 1) * LANES] = o2.T.astype(o_ref.dtype)

    ctx_chunk = (lambda: kc_ref[...], lambda: vtc_ref[...])
    lat_chunks = [
        (functools.partial(lambda j: kl_ref[j * tk:(j + 1) * tk, :], j),
         functools.partial(lambda j: vtl_ref[:, j * tk:(j + 1) * tk], j))
        for j in range(n_lat_chunks)]

    @pl.when(qt < nq_lat)
    def _():
        run([ctx_chunk] + lat_chunks)

    if ctx_queries:
        @pl.when(qt >= nq_lat)
        def _():
            run([ctx_chunk])


def _attention(q, k, vt, *, B, S, L, shared_k, ctx_queries):
    T = q.shape[0]
    tq = L
    tk = min(512, S)
    nq_lat = S // tq
    nq = nq_lat + (1 if ctx_queries else 0)
    ctx_blk0 = B * S // L

    def qrow(b, qt):
        return jnp.where(qt < nq_lat, b * nq_lat + qt, ctx_blk0 + b)

    if shared_k:
        units, heads, out_w = A_KV_HEADS, A_HEADS // A_KV_HEADS, 2 * LANES
        q_spec = pl.BlockSpec((tq, 2 * LANES), lambda b, p, qt: (qrow(b, qt), p))
        kl_spec = pl.BlockSpec((None, S, LANES), lambda b, p, qt: (p, b, 0))
        kc_spec = pl.BlockSpec((None, L, LANES), lambda b, p, qt: (p, ctx_blk0 + b, 0))
        vl_spec = pl.BlockSpec((None, VT_ROWS, S), lambda b, p, qt: (p, 0, b))
        vc_spec = pl.BlockSpec((None, VT_ROWS, L), lambda b, p, qt: (p, 0, ctx_blk0 + b))
    else:
        units, heads, out_w = M_HEADS // 2, 2, LANES
        q_spec = pl.BlockSpec((tq, 2 * LANES), lambda b, p, qt: (qrow(b, qt), p))
        kl_spec = pl.BlockSpec((S, 2 * LANES), lambda b, p, qt: (b, p))
        kc_spec = pl.BlockSpec((L, 2 * LANES), lambda b, p, qt: (ctx_blk0 + b, p))
        vl_spec = pl.BlockSpec((None, VT_ROWS, S), lambda b, p, qt: (p, 0, b))
        vc_spec = pl.BlockSpec((None, VT_ROWS, L), lambda b, p, qt: (p, 0, ctx_blk0 + b))
    return pl.pallas_call(
        functools.partial(_attn_kernel, shared_k=shared_k, tq=tq, tk=tk,
                          n_lat_chunks=S // tk, nq_lat=nq_lat, ctx_queries=ctx_queries),
        grid=(B, units, nq),
        in_specs=[q_spec, kl_spec, kc_spec, vl_spec, vc_spec],
        out_specs=pl.BlockSpec((tq, out_w), lambda b, p, qt: (qrow(b, qt), p)),
        out_shape=jax.ShapeDtypeStruct((T if ctx_queries else B * S, units * out_w), BF16),
        scratch_shapes=[pltpu.VMEM((VT_ROWS, heads * tq), F32)],
        compiler_params=_cparams(3),
        name="attention",
    )(q, k, k, vt, vt)


def _linear_resid_kernel(a_ref, w_ref, x_ref, mod_ref, g_ref, o_ref, *, gate_row):
    y = _dot(a_ref[...], w_ref[...])
    o_ref[...] = x_ref[...] + mod_ref[gate_row:gate_row + 1, :] * _rms(y, g_ref[...])


def _linear_resid(A, W, X, mod, g, geom, *, gate_row, n_tiles):
    tm = geom.tm
    return pl.pallas_call(
        functools.partial(_linear_resid_kernel, gate_row=gate_row),
        grid=(n_tiles,),
        in_specs=[
            _row_spec(tm, A.shape[1]), _full_spec(W.shape), _row_spec(tm, D_MODEL),
            geom.mod_spec(), _full_spec((1, D_MODEL)),
        ],
        out_specs=_row_spec(tm, D_MODEL),
        out_shape=jax.ShapeDtypeStruct((n_tiles * tm, D_MODEL), F32),
        compiler_params=_cparams(1),
        name="linear_resid",
    )(A, W, X, mod, g)


def _norm_linear_kernel(x_ref, mod_ref, g_ref, w_ref, o_ref):
    h = _norm_mod(x_ref[...], g_ref[...], mod_ref[0:1, :], mod_ref[1:2, :]).astype(BF16)
    o_ref[...] = _dot(h, w_ref[...]).astype(o_ref.dtype)


def _norm_linear(X, mod, g, W, geom):
    tm = geom.tm
    n = W.shape[1]
    return pl.pallas_call(
        _norm_linear_kernel,
        grid=(geom.n_all,),
        in_specs=[_row_spec(tm, D_MODEL), geom.mod_spec(), _full_spec((1, D_MODEL)),
                  _full_spec(W.shape)],
        out_specs=_row_spec(tm, n),
        out_shape=jax.ShapeDtypeStruct((X.shape[0], n), BF16),
        compiler_params=_cparams(1),
        name="norm_linear",
    )(X, mod, g, W)


def _seq_dft_kernel(c_ref, s_ref, y1_ref, y2_ref, o_ref, acc_ref):
    k = pl.program_id(2)

    @pl.when(k == 0)
    def _():
        acc_ref[...] = jnp.zeros(acc_ref.shape, F32)

    acc_ref[...] += _dot(c_ref[...], y1_ref[...]) + _dot(s_ref[...], y2_ref[...])

    @pl.when(k == pl.num_programs(2) - 1)
    def _():
        o_ref[...] = acc_ref[...].astype(o_ref.dtype)


def _seq_dft(cmat, smat, Y, *, n_batch, n_seq, row_blk0):
    tu = min(1024, n_seq)
    nu = n_seq // tu
    return pl.pallas_call(
        _seq_dft_kernel,
        grid=(n_batch, nu, nu),
        in_specs=[
            pl.BlockSpec((tu, tu), lambda b, u, k: (u, k)),
            pl.BlockSpec((tu, tu), lambda b, u, k: (u, k)),
            pl.BlockSpec((tu, D_MODEL), lambda b, u, k: (row_blk0 + b * nu + k, 0)),
            pl.BlockSpec((tu, D_MODEL), lambda b, u, k: (row_blk0 + b * nu + k, 1)),
        ],
        out_specs=pl.BlockSpec((tu, D_MODEL), lambda b, u, k: (b * nu + u, 0)),
        out_shape=jax.ShapeDtypeStruct((n_batch * n_seq, D_MODEL), BF16),
        scratch_shapes=[pltpu.VMEM((tu, D_MODEL), F32)],
        compiler_params=_cparams(3),
        name="seq_dft",
    )(cmat, smat, Y, Y)


def _swiglu_chunks(h, wgu_ref, wd_ref, acc_ref, ff, tf, between=None):
    n = ff // tf
    for c in range(n):
        g = _dot(h, wgu_ref[:, c * tf:(c + 1) * tf])
        u = _dot(h, wgu_ref[:, ff + c * tf:ff + (c + 1) * tf])
        a = ((g * jax.nn.sigmoid(g)) * u).astype(BF16)
        part = _dot(a, wd_ref[c * tf:(c + 1) * tf, :])
        if c == 0:
            acc_ref[...] = part
        else:
            acc_ref[...] += part
        if between is not None:
            between(c, n)


def _ffn_dense_kernel(x_ref, mod_ref, g2_ref, g3_ref, wgu_ref, wd_ref, o_ref, acc_ref,
                      *, ff, tf):
    x = x_ref[...]
    h = _norm_mod(x, g2_ref[...], mod_ref[3:4, :], mod_ref[4:5, :]).astype(BF16)
    _swiglu_chunks(h, wgu_ref, wd_ref, acc_ref, ff, tf)
    o_ref[...] = x + mod_ref[5:6, :] * _rms(acc_ref[...], g3_ref[...])


def _ffn_dense(X, mod, g2, g3, wgu, wd, geom):
    tm = geom.tm
    ff = wd.shape[0]
    single = pl.Buffered(1)
    return pl.pallas_call(
        functools.partial(_ffn_dense_kernel, ff=ff, tf=256),
        grid=(geom.n_all,),
        in_specs=[
            _row_spec(tm, D_MODEL), geom.mod_spec(), _full_spec((1, D_MODEL)),
            _full_spec((1, D_MODEL)),
            pl.BlockSpec(wgu.shape, lambda i: (0, 0), pipeline_mode=single),
            pl.BlockSpec(wd.shape, lambda i: (0, 0), pipeline_mode=single),
        ],
        out_specs=_row_spec(tm, D_MODEL),
        out_shape=jax.ShapeDtypeStruct(X.shape, F32),
        scratch_shapes=[pltpu.VMEM((tm, D_MODEL), F32)],
        compiler_params=_cparams(1),
        name="ffn_dense",
    )(X, mod, g2, g3, wgu, wd)


def _router_kernel(x_ref, mod_ref, g_ref, wr_ref, h_ref, r_ref):
    h = _norm_mod(x_ref[...], g_ref[...], mod_ref[3:4, :], mod_ref[4:5, :])
    h_ref[...] = h
    logits = jnp.dot(h, wr_ref[...], precision=lax.Precision.HIGHEST,
                     preferred_element_type=F32)
    lane = lax.broadcasted_iota(jnp.int32, logits.shape, 1)
    lane_f = lane.astype(F32)
    neg = jnp.float32(-jnp.inf)
    logits = jnp.where(lane < N_EXPERTS, logits, neg)
    v1 = jnp.max(logits, axis=-1, keepdims=True)
    i1 = jnp.min(jnp.where(logits == v1, lane_f, float(LANES)), axis=-1, keepdims=True)
    rest = jnp.where(lane_f == i1, neg, logits)
    v2 = jnp.max(rest, axis=-1, keepdims=True)
    i2 = jnp.min(jnp.where(rest == v2, lane_f, float(LANES)), axis=-1, keepdims=True)
    e2 = jnp.exp(v2 - v1)
    den = 1.0 + e2
    w1 = 1.0 / den
    w2 = e2 / den
    out = jnp.where(lane == 0, w1, 0.0)
    out = jnp.where(lane == 1, w2, out)
    out = jnp.where(lane == 2, i1, out)
    out = jnp.where(lane == 3, i2, out)
    r_ref[...] = out


def _router(X, mod, g, wr, geom, n_tiles):
    tm = geom.tm
    rows = n_tiles * tm
    return pl.pallas_call(
        _router_kernel,
        grid=(n_tiles,),
        in_specs=[_row_spec(tm, D_MODEL), geom.mod_spec(), _full_spec((1, D_MODEL)),
                  _full_spec(wr.shape)],
        out_specs=[_row_spec(tm, D_MODEL), _row_spec(tm, LANES)],
        out_shape=[jax.ShapeDtypeStruct((rows, D_MODEL), F32),
                   jax.ShapeDtypeStruct((rows, LANES), F32)],
        compiler_params=_cparams(1),
        name="moe_router",
    )(X, mod, g, wr)


GATHER_UNROLL = 8


def _start_row(src_hbm, idx_ref, dst, sem, r, priority):
    t = idx_ref[0, 0, r]
    pltpu.make_async_copy(src_hbm.at[pl.ds(t, 1), :], dst.at[pl.ds(r, 1), :], sem).start(
        priority=priority)


def _start_row_gather(src_hbm, idx_ref, dst, sem, n_rows):
    def body(g, carry):
        for u in range(GATHER_UNROLL):
            _start_row(src_hbm, idx_ref, dst, sem, g * GATHER_UNROLL + u, u % 2)
        return carry

    lax.fori_loop(0, n_rows // GATHER_UNROLL, body, 0)


def _start_rows_static(src_hbm, idx_ref, dst, sem, r0, r1):
    for r in range(r0, r1):
        _start_row(src_hbm, idx_ref, dst, sem, r, r % 2)


def _wait_row_gather(src_hbm, dst, sem, n_rows):
    pltpu.make_async_copy(src_hbm.at[pl.ds(0, n_rows), :], dst, sem).wait()


def _moe_ffn_kernel(te_ref, nu_ref, idx_ref, idx_next_ref, h_hbm, wgu_ref, wd_ref, o_ref,
                    gbuf, sem, acc_ref, *, tm, ff, tf):
    i = pl.program_id(0)
    n_used = nu_ref[0]
    slot = i % 2

    @pl.when(i == 0)
    def _():
        _start_row_gather(h_hbm, idx_ref, gbuf.at[0], sem.at[0], tm)

    _wait_row_gather(h_hbm, gbuf.at[slot], sem.at[slot], tm)

    @pl.when(i < n_used)
    def _():
        h = gbuf[slot].astype(BF16)

        def issue_next(c, n):
            _start_rows_static(h_hbm, idx_next_ref, gbuf.at[1 - slot], sem.at[1 - slot],
                               c * tm // n, (c + 1) * tm // n)

        _swiglu_chunks(h, wgu_ref, wd_ref, acc_ref, ff, tf, issue_next)
        o_ref[...] = acc_ref[...]

    @pl.when(i >= n_used)
    def _():
        _start_row_gather(h_hbm, idx_next_ref, gbuf.at[1 - slot], sem.at[1 - slot], tm)
        o_ref[...] = jnp.zeros(o_ref.shape, o_ref.dtype)

    @pl.when(i == pl.num_programs(0) - 1)
    def _():
        _wait_row_gather(h_hbm, gbuf.at[1 - slot], sem.at[1 - slot], tm)


def _moe_ffn(h2, wgu, wd, tile_expert, n_used, slot_token, *, tm, n_tiles):
    ff = wd.shape[1]
    single = pl.Buffered(1)
    grid_spec = pltpu.PrefetchScalarGridSpec(
        num_scalar_prefetch=2,
        grid=(n_tiles,),
        in_specs=[
            pl.BlockSpec((1, 1, tm), lambda i, te, nu: (i, 0, 0), memory_space=pltpu.SMEM),
            pl.BlockSpec((1, 1, tm), lambda i, te, nu: (jnp.minimum(i + 1, n_tiles - 1), 0, 0),
                         memory_space=pltpu.SMEM),
            pl.BlockSpec(memory_space=pl.ANY),
            pl.BlockSpec((None, D_MODEL, 2 * ff), lambda i, te, nu: (te[i], 0, 0),
                         pipeline_mode=single),
            pl.BlockSpec((None, ff, D_MODEL), lambda i, te, nu: (te[i], 0, 0),
                         pipeline_mode=single),
        ],
        out_specs=pl.BlockSpec((tm, D_MODEL), lambda i, te, nu: (i, 0)),
        scratch_shapes=[
            pltpu.VMEM((2, tm, D_MODEL), F32),
            pltpu.SemaphoreType.DMA((2,)),
            pltpu.VMEM((tm, D_MODEL), F32),
        ],
    )
    return pl.pallas_call(
        functools.partial(_moe_ffn_kernel, tm=tm, ff=ff, tf=512),
        grid_spec=grid_spec,
        out_shape=jax.ShapeDtypeStruct((n_tiles * tm, D_MODEL), F32),
        compiler_params=_cparams(1),
        name="moe_ffn",
    )(tile_expert, n_used, slot_token, slot_token, h2, wgu, wd)


def _moe_combine_kernel(idx_ref, idx_next_ref, y_hbm, r_ref, x_ref, mod_ref, g_ref, o_ref,
                        gbuf, sem, *, tm):
    i = pl.program_id(0)
    n = pl.num_programs(0)
    slot = i % 2

    @pl.when(i == 0)
    def _():
        _start_row_gather(y_hbm, idx_ref, gbuf.at[0], sem.at[0], 2 * tm)

    _wait_row_gather(y_hbm, gbuf.at[slot], sem.at[slot], 2 * tm)

    @pl.when(i + 1 < n)
    def _():
        _start_row_gather(y_hbm, idx_next_ref, gbuf.at[1 - slot], sem.at[1 - slot], 2 * tm)

    r = r_ref[...]
    y = r[:, 0:1] * gbuf[slot, 0:tm, :] + r[:, 1:2] * gbuf[slot, tm:2 * tm, :]
    o_ref[...] = x_ref[...] + mod_ref[5:6, :] * _rms(y, g_ref[...])


def _moe_combine(Y, tok_slot, route, X, mod, g, geom, n_tiles):
    tm = geom.tm
    return pl.pallas_call(
        functools.partial(_moe_combine_kernel, tm=tm),
        grid=(n_tiles,),
        in_specs=[
            pl.BlockSpec((1, 1, 2 * tm), lambda i: (i, 0, 0), memory_space=pltpu.SMEM),
            pl.BlockSpec((1, 1, 2 * tm), lambda i: (jnp.minimum(i + 1, n_tiles - 1), 0, 0),
                         memory_space=pltpu.SMEM),
            pl.BlockSpec(memory_space=pl.ANY),
            _row_spec(tm, LANES), _row_spec(tm, D_MODEL), geom.mod_spec(),
            _full_spec((1, D_MODEL)),
        ],
        out_specs=_row_spec(tm, D_MODEL),
        out_shape=jax.ShapeDtypeStruct((n_tiles * tm, D_MODEL), F32),
        scratch_shapes=[pltpu.VMEM((2, 2 * tm, D_MODEL), F32), pltpu.SemaphoreType.DMA((2,))],
        compiler_params=_cparams(1),
        name="moe_combine",
    )(tok_slot, tok_slot, Y, route, X, mod, g)


def _moe_layer(X, mod, g2, g3, w_router, wgu, wd, geom, n_tiles, tm_e):
    tm = geom.tm
    rows = n_tiles * tm
    wr = jnp.pad(w_router, ((0, 0), (0, LANES - N_EXPERTS)))
    h2, route = _router(X, mod, g2, wr, geom, n_tiles)

    e_flat = route[:, 2:4].astype(jnp.int32).reshape(-1)
    onehot = (e_flat[:, None] == jnp.arange(N_EXPERTS, dtype=jnp.int32)[None, :]).astype(jnp.int32)
    csum = jnp.cumsum(onehot, axis=0)
    rank = jnp.sum(csum * onehot, axis=1) - 1
    counts = csum[-1]
    tiles_e = (counts + tm_e - 1) // tm_e
    tile_end = jnp.cumsum(tiles_e)
    tile_start = tile_end - tiles_e
    n_used = tile_end[-1]
    slot = tile_start[e_flat] * tm_e + rank
    n_tiles_e = (2 * rows) // tm_e + N_EXPERTS
    tok = jnp.arange(2 * rows, dtype=jnp.int32) // 2
    slot_token = jnp.zeros((n_tiles_e * tm_e,), jnp.int32).at[slot].set(tok)
    tile_ids = jnp.minimum(jnp.arange(n_tiles_e, dtype=jnp.int32), n_used - 1)
    tile_expert = jnp.sum((tile_ids[:, None] >= tile_end[None, :]).astype(jnp.int32), axis=1)
    tile_expert = jnp.minimum(tile_expert, N_EXPERTS - 1).astype(jnp.int32)

    Y = _moe_ffn(h2, wgu, wd, tile_expert, n_used.reshape(1).astype(jnp.int32),
                 slot_token.reshape(n_tiles_e, 1, tm_e), tm=tm_e, n_tiles=n_tiles_e)
    slot2 = slot.reshape(rows, 2)
    tok_slot = jnp.concatenate(
        [slot2[:, 0].reshape(n_tiles, 1, tm), slot2[:, 1].reshape(n_tiles, 1, tm)], axis=2)
    return _moe_combine(Y, tok_slot.astype(jnp.int32), route, X, mod, g3, geom, n_tiles)


def _rope_tables(S, tm, rot_dim, lane0, reps):
    nf = rot_dim // 4
    t = jnp.arange(S, dtype=jnp.int32)
    rows = (t // GRID_W).astype(F32)
    cols = (t % GRID_W).astype(F32)
    inv = ROPE_BASE ** (-jnp.arange(nf, dtype=F32) / nf)
    ang_r = rows[:, None] * inv[None, :]
    ang_c = cols[:, None] * inv[None, :]
    cr, sr, cc, sc = jnp.cos(ang_r), jnp.sin(ang_r), jnp.cos(ang_c), jnp.sin(ang_c)
    cos_blk = jnp.concatenate([cr, cr, cc, cc], axis=1)
    sin_blk = jnp.concatenate([-sr, sr, -sc, sc], axis=1)
    cos = jnp.ones((S, LANES), F32)
    sin = jnp.zeros((S, LANES), F32)
    for r in range(reps):
        a = lane0 + r * rot_dim
        cos = cos.at[:, a:a + rot_dim].set(cos_blk)
        sin = sin.at[:, a:a + rot_dim].set(sin_blk)
    cos = jnp.concatenate([cos, jnp.ones((tm, LANES), F32)], axis=0)
    sin = jnp.concatenate([sin, jnp.zeros((tm, LANES), F32)], axis=0)
    return cos, sin


def _dft_mats(n, scale):
    u = jnp.arange(n, dtype=jnp.int32)
    idx = (u[:, None] * u[None, :]) % n
    ang = idx.astype(F32) * (2.0 * math.pi / n)
    return (jnp.cos(ang) * scale).astype(BF16), (-jnp.sin(ang) * scale).astype(BF16)


def _channel_dft_weight():
    n = D_MODEL // F_GROUPS
    v = jnp.arange(n, dtype=jnp.int32)
    ang = ((v[:, None] * v[None, :]) % n).astype(F32) * (2.0 * math.pi / n)
    eye = jnp.eye(F_GROUPS, dtype=F32)
    wc = jnp.kron(eye, jnp.cos(ang))
    ws = jnp.kron(eye, jnp.sin(ang))
    return jnp.concatenate([wc, ws], axis=1).astype(BF16)


def kernel(x, c, ctx, c_ctx, w_mod, b_mod, norm_g, a_w_qkv, a_g_q, a_g_k, a_w_o, f_w,
           m_w_down, m_g_cq, m_g_ckv, m_w_uq, m_w_ukv, m_w_o, d_w_gu, d_w_down,
           e_w_router, e_w_gu, e_w_down):
    B, S, D = x.shape
    L = ctx.shape[1]
    depth = w_mod.shape[0]
    assert D == D_MODEL and B + 1 <= N_MOD_ROWS and S % L == 0
    tm = min(512, S)
    geom = _Geom(B, S, L, tm)
    n_lat, n_all = geom.n_lat, geom.n_all

    X = jnp.concatenate([x.reshape(B * S, D), ctx.reshape(B * L, D)], axis=0)
    cvec = jnp.zeros((N_MOD_ROWS, D), F32).at[:B].set(c).at[B].set(c_ctx)
    mods = _mod_vectors(cvec, w_mod, b_mod)

    rope_a = _rope_tables(S, tm, A_HEAD_DIM, 0, 2)
    rope_m = _rope_tables(S, tm, M_ROPE, M_NOPE, 1)

    for i in range(depth):
        last = i == depth - 1
        kind = i % N_MIXERS
        j = i // N_MIXERS
        mod = mods[i]
        g = norm_g[i].reshape(4, 1, D)
        n_out = n_lat if last else n_all

        if kind == 0:
            gq = jnp.tile(a_g_q[j], 2).reshape(1, LANES)
            gk = jnp.tile(a_g_k[j], 2).reshape(1, LANES)
            q, kd, vd = _gqa_proj(X, mod, g[0], a_w_qkv[j].astype(BF16), gq, gk,
                                  rope_a[0], rope_a[1], geom)
            att = _attention(q, kd, vd, B=B, S=S, L=L, shared_k=True, ctx_queries=not last)
            X = _linear_resid(att, a_w_o[j].astype(BF16), X, mod, g[1], geom,
                              gate_row=2, n_tiles=n_out)
        elif kind == 1:
            Y = _norm_linear(X, mod, g[0], _channel_dft_weight(), geom)
            n_ch = D // F_GROUPS
            cl, sl = _dft_mats(S, 1.0 / math.sqrt(S * n_ch))
            Z = _seq_dft(cl, sl, Y, n_batch=B, n_seq=S, row_blk0=0)
            if not last:
                cc_, sc_ = _dft_mats(L, 1.0 / math.sqrt(L * n_ch))
                Zc = _seq_dft(cc_, sc_, Y, n_batch=B, n_seq=L, row_blk0=B * S // L)
                Z = jnp.concatenate([Z, Zc], axis=0)
            X = _linear_resid(Z, f_w[j].astype(BF16), X, mod, g[1], geom,
                              gate_row=2, n_tiles=n_out)
        else:
            wd = m_w_down[j]
            wdn = jnp.concatenate([
                wd[:, :M_Q_LORA + M_KV_LORA],
                jnp.zeros((D, M_NOPE), F32), wd[:, M_Q_LORA + M_KV_LORA:],
                jnp.zeros((D, LANES - M_NOPE - M_ROPE), F32)], axis=1).astype(BF16)
            wuq = jnp.pad(m_w_uq[j].reshape(M_Q_LORA, M_HEADS, M_NOPE + M_ROPE),
                          ((0, 0), (0, 0), (0, LANES - M_NOPE - M_ROPE)))
            wuq = wuq.reshape(M_Q_LORA, M_HEADS * LANES).astype(BF16)
            wukv = m_w_ukv[j].reshape(M_KV_LORA, M_HEADS, M_NOPE + M_V)
            wuk = jnp.pad(wukv[:, :, :M_NOPE], ((0, 0), (0, 0), (0, LANES - M_NOPE)))
            wuk = wuk.reshape(M_KV_LORA, M_HEADS * LANES).astype(BF16)
            wuv = wukv[:, :, M_NOPE:].reshape(M_KV_LORA, M_HEADS * M_V).astype(BF16)
            q, k, v = _mla_proj(X, mod, g[0], wdn, m_g_cq[j].reshape(1, -1),
                                m_g_ckv[j].reshape(1, -1), wuq, wuk, wuv,
                                rope_m[0], rope_m[1], geom)
            att = _attention(q, k, v, B=B, S=S, L=L, shared_k=False, ctx_queries=not last)
            X = _linear_resid(att, m_w_o[j].astype(BF16), X, mod, g[1], geom,
                              gate_row=2, n_tiles=n_out)

        if i % 2 == 0:
            assert not last
            X = _ffn_dense(X, mod, g[2], g[3], d_w_gu[i // 2].astype(BF16),
                           d_w_down[i // 2].astype(BF16), geom)
        else:
            X = _moe_layer(X, mod, g[2], g[3], e_w_router[i // 2],
                           e_w_gu[i // 2].astype(BF16), e_w_down[i // 2].astype(BF16),
                           geom, n_out, min(512, S))

    return X[:B * S].reshape(B, S, D)
```

```python
import functools
import math

import jax
import jax.numpy as jnp
from jax import lax
from jax.experimental import pallas as pl
from jax.experimental.pallas import tpu as pltpu

F32 = jnp.float32
BF16 = jnp.bfloat16

D_MODEL = 1024
GRID_W = 64
NORM_EPS = 1e-6
ROPE_BASE = 10000.0
LANES = 128
HALF = LANES // 2

A_HEADS = 16
A_KV_HEADS = 4
A_HEAD_DIM = 64
F_GROUPS = 4
M_HEADS = 16
M_Q_LORA = 384
M_KV_LORA = 256
M_NOPE = 64
M_ROPE = 32
M_V = 64
N_EXPERTS = 8
N_MIXERS = 3

VT_ROWS = LANES + 16
LOG2_E = math.log2(math.e)
VMEM_LIMIT = 56 * 1024 * 1024
N_MOD_ROWS = 16


def _cparams(n_axes):
    return pltpu.CompilerParams(
        dimension_semantics=("arbitrary",) * n_axes, vmem_limit_bytes=VMEM_LIMIT)


def _rms(x, g):
    y = x * lax.rsqrt(jnp.mean(x * x, axis=-1, keepdims=True) + NORM_EPS)
    return y * g


def _norm_mod(x, g, shift, scale):
    return _rms(x, g) * (1.0 + scale) + shift


def _half_rms(y, g):
    lo = lax.broadcasted_iota(jnp.int32, y.shape, 1) < HALF
    s = y * y
    s_lo = jnp.sum(jnp.where(lo, s, 0.0), axis=-1, keepdims=True)
    s_hi = jnp.sum(jnp.where(lo, 0.0, s), axis=-1, keepdims=True)
    ms = jnp.where(lo, s_lo, s_hi) * (1.0 / HALF)
    return (y * lax.rsqrt(ms + NORM_EPS)) * g


def _rope_chunk(y, cos, sin, width):
    lane = lax.broadcasted_iota(jnp.int32, y.shape, 1)
    first = ((lane // width) % 2) == 0
    partner = jnp.where(first, pltpu.roll(y, LANES - width, 1), pltpu.roll(y, width, 1))
    return y * cos + partner * sin


def _dup_halves(y):
    lo = lax.broadcasted_iota(jnp.int32, y.shape, 1) < HALF
    r = pltpu.roll(y, HALF, 1)
    return jnp.where(lo, y, r), jnp.where(lo, r, y)


def _dot(a, b):
    return jnp.dot(a, b, preferred_element_type=F32)


def _mod_kernel(c_ref, w_ref, b_ref, o_ref):
    c = c_ref[...]
    sc = c * jax.nn.sigmoid(c)
    o_ref[...] = jnp.dot(sc, w_ref[...], precision=lax.Precision.HIGHEST,
                         preferred_element_type=F32) + b_ref[...]


def _mod_vectors(cvec, w_mod, b_mod):
    depth, d, n = w_mod.shape
    tn = 1536
    out = pl.pallas_call(
        _mod_kernel,
        grid=(depth, n // tn),
        in_specs=[
            pl.BlockSpec((N_MOD_ROWS, d), lambda i, j: (0, 0)),
            pl.BlockSpec((None, d, tn), lambda i, j: (i, 0, j)),
            pl.BlockSpec((None, 1, tn), lambda i, j: (i, 0, j)),
        ],
        out_specs=pl.BlockSpec((None, N_MOD_ROWS, tn), lambda i, j: (i, 0, j)),
        out_shape=jax.ShapeDtypeStruct((depth, N_MOD_ROWS, n), F32),
        compiler_params=_cparams(2),
        name="mod_vectors",
    )(cvec, w_mod, b_mod.reshape(depth, 1, n))
    return out.reshape(depth, N_MOD_ROWS, 6, d)


class _Geom:
    def __init__(self, B, S, L, tm):
        assert S % tm == 0 and (B * L) % tm == 0
        self.B, self.S, self.L, self.tm = B, S, L, tm
        self.n_lat = B * S // tm
        self.n_all = (B * S + B * L) // tm
        self.per_batch = S // tm

    def mod_map(self, i):
        return (jnp.where(i < self.n_lat, i // self.per_batch, self.B), 0, 0)

    def table_map(self, i):
        return (jnp.where(i < self.n_lat, i % self.per_batch, self.per_batch), 0)

    def mod_spec(self):
        return pl.BlockSpec((None, 6, D_MODEL), lambda i: self.mod_map(i))


def _row_spec(tm, n):
    return pl.BlockSpec((tm, n), lambda i: (i, 0))


def _full_spec(shape):
    nd = len(shape)
    return pl.BlockSpec(shape, lambda i: (0,) * nd)


def _gqa_proj_kernel(x_ref, mod_ref, g_ref, w_ref, gq_ref, gk_ref, cos_ref, sin_ref,
                     q_ref, k_ref, vt_ref, *, scale):
    h = _norm_mod(x_ref[...], g_ref[...], mod_ref[0:1, :], mod_ref[1:2, :]).astype(BF16)
    qkv = _dot(h, w_ref[...])
    cos = cos_ref[...]
    sin = sin_ref[...]
    nq = A_HEADS * A_HEAD_DIM // LANES
    nk = A_KV_HEADS * A_HEAD_DIM // LANES
    for c in range(nq):
        y = _half_rms(qkv[:, c * LANES:(c + 1) * LANES], gq_ref[...])
        y = _rope_chunk(y, cos, sin, A_HEAD_DIM // 4) * scale
        q_ref[:, c * LANES:(c + 1) * LANES] = y.astype(BF16)
    for c in range(nk):
        y = _half_rms(qkv[:, (nq + c) * LANES:(nq + c + 1) * LANES], gk_ref[...])
        y = _rope_chunk(y, cos, sin, A_HEAD_DIM // 4)
        a, b = _dup_halves(y)
        k_ref[2 * c] = a.astype(BF16)
        k_ref[2 * c + 1] = b.astype(BF16)
    for c in range(nk):
        a, b = _dup_halves(qkv[:, (nq + nk + c) * LANES:(nq + nk + c + 1) * LANES])
        vt_ref[2 * c, :LANES, :] = a.T.astype(BF16)
        vt_ref[2 * c + 1, :LANES, :] = b.T.astype(BF16)
    vt_ref[:, LANES:, :] = jnp.ones((A_KV_HEADS, VT_ROWS - LANES, vt_ref.shape[2]), BF16)


def _gqa_proj(X, mod, g, w_qkv, gq, gk, cos, sin, geom):
    tm = geom.tm
    T = X.shape[0]
    nqkv = w_qkv.shape[1]
    kv_spec = pl.BlockSpec((A_KV_HEADS, tm, LANES), lambda i: (0, i, 0))
    tab_spec = pl.BlockSpec((tm, LANES), lambda i: geom.table_map(i))
    return pl.pallas_call(
        functools.partial(_gqa_proj_kernel, scale=A_HEAD_DIM ** -0.5 * LOG2_E),
        grid=(geom.n_all,),
        in_specs=[
            _row_spec(tm, D_MODEL), geom.mod_spec(), _full_spec((1, D_MODEL)),
            _full_spec((D_MODEL, nqkv)), _full_spec((1, LANES)), _full_spec((1, LANES)),
            tab_spec, tab_spec,
        ],
        out_specs=[_row_spec(tm, A_HEADS * A_HEAD_DIM), kv_spec,
                   pl.BlockSpec((A_KV_HEADS, VT_ROWS, tm), lambda i: (0, 0, i))],
        out_shape=[
            jax.ShapeDtypeStruct((T, A_HEADS * A_HEAD_DIM), BF16),
            jax.ShapeDtypeStruct((A_KV_HEADS, T, LANES), BF16),
            jax.ShapeDtypeStruct((A_KV_HEADS, VT_ROWS, T), BF16),
        ],
        compiler_params=_cparams(1),
        name="gqa_proj",
    )(X, mod, g, w_qkv, gq, gk, cos, sin)


def _mla_proj_kernel(x_ref, mod_ref, g_ref, wdn_ref, gcq_ref, gckv_ref, wuq_ref, wuk_ref,
                     wuv_ref, cos_ref, sin_ref, q_ref, k_ref, vt_ref, *, scale):
    h = _norm_mod(x_ref[...], g_ref[...], mod_ref[0:1, :], mod_ref[1:2, :]).astype(BF16)
    d = _dot(h, wdn_ref[...])
    cos = cos_ref[...]
    sin = sin_ref[...]
    cq = _rms(d[:, :M_Q_LORA], gcq_ref[...]).astype(BF16)
    ckv = _rms(d[:, M_Q_LORA:M_Q_LORA + M_KV_LORA], gckv_ref[...]).astype(BF16)
    kr = _rope_chunk(d[:, M_Q_LORA + M_KV_LORA:], cos, sin, M_ROPE // 4)
    q = _dot(cq, wuq_ref[...])
    kn = _dot(ckv, wuk_ref[...])
    for c in range(M_HEADS):
        sl = slice(c * LANES, (c + 1) * LANES)
        q_ref[:, sl] = (_rope_chunk(q[:, sl], cos, sin, M_ROPE // 4) * scale).astype(BF16)
        k_ref[:, sl] = (kn[:, sl] + kr).astype(BF16)
    v = _dot(ckv, wuv_ref[...])
    nv = M_HEADS * M_V
    for c in range(nv // LANES):
        sl = slice(c * LANES, (c + 1) * LANES)
        vt_ref[c, :LANES, :] = v[:, sl].T.astype(BF16)
    vt_ref[:, LANES:, :] = jnp.ones((nv // LANES, VT_ROWS - LANES, vt_ref.shape[2]), BF16)


def _mla_proj(X, mod, g, wdn, gcq, gckv, wuq, wuk, wuv, cos, sin, geom):
    tm = geom.tm
    T = X.shape[0]
    tab_spec = pl.BlockSpec((tm, LANES), lambda i: geom.table_map(i))
    nq = M_HEADS * LANES
    nv = M_HEADS * M_V
    return pl.pallas_call(
        functools.partial(_mla_proj_kernel, scale=(M_NOPE + M_ROPE) ** -0.5 * LOG2_E),
        grid=(geom.n_all,),
        in_specs=[
            _row_spec(tm, D_MODEL), geom.mod_spec(), _full_spec((1, D_MODEL)),
            _full_spec(wdn.shape), _full_spec((1, M_Q_LORA)), _full_spec((1, M_KV_LORA)),
            _full_spec(wuq.shape), _full_spec(wuk.shape), _full_spec(wuv.shape),
            tab_spec, tab_spec,
        ],
        out_specs=[_row_spec(tm, nq), _row_spec(tm, nq),
                   pl.BlockSpec((nv // LANES, VT_ROWS, tm), lambda i: (0, 0, i))],
        out_shape=[
            jax.ShapeDtypeStruct((T, nq), BF16),
            jax.ShapeDtypeStruct((T, nq), BF16),
            jax.ShapeDtypeStruct((nv // LANES, VT_ROWS, T), BF16),
        ],
        compiler_params=_cparams(1),
        name="mla_proj",
    )(X, mod, g, wdn, gcq, gckv, wuq, wuk, wuv, cos, sin)


def _attn_kernel(q_ref, kl_ref, kc_ref, vtl_ref, vtc_ref, o_ref, acc_ref,
                 *, shared_k, tq, tk, n_lat_chunks, nq_lat, ctx_queries):
    qt = pl.program_id(2)
    q = q_ref[...]
    if shared_k:
        lo = lax.broadcasted_iota(jnp.int32, (tq, LANES), 1) < HALF
        zero = jnp.zeros((tq, LANES), q.dtype)
        pieces = []
        for c in range(q.shape[1] // LANES):
            qc = q[:, c * LANES:(c + 1) * LANES]
            pieces += [jnp.where(lo, qc, zero), jnp.where(lo, zero, qc)]
        qs = jnp.concatenate(pieces, axis=0)

    nt = (((1,), (1,)), ((), ()))

    def scores_t(k):
        if shared_k:
            return lax.dot_general(k, qs, nt, preferred_element_type=F32)
        return jnp.concatenate([
            lax.dot_general(k[:, c * LANES:(c + 1) * LANES], q[:, c * LANES:(c + 1) * LANES], nt,
                            preferred_element_type=F32)
            for c in range(q.shape[1] // LANES)], axis=1)

    def pv(vt, p):
        if vt.ndim == 2:
            return _dot(vt, p)
        return jnp.concatenate([
            _dot(vt[u], p[:, 2 * u * tq:(2 * u + 2) * tq]) for u in range(vt.shape[0])], axis=1)

    def soft_pv(s, vt, m_prev):
        m_cur = jnp.max(s, axis=0, keepdims=True)
        if m_prev is None:
            p = jnp.exp2(s - m_cur)
            acc_ref[...] = pv(vt, p.astype(BF16))
            return m_cur
        m_new = jnp.maximum(m_prev, m_cur)
        alpha = jnp.exp2(m_prev - m_new)
        p = jnp.exp2(s - m_new)
        acc_ref[...] = alpha * acc_ref[...] + pv(vt, p.astype(BF16))
        return m_new

    def run(chunks):
        s_next = scores_t(chunks[0][0]())
        m = None
        for j, (_, get_vt) in enumerate(chunks):
            s = s_next
            if j + 1 < len(chunks):
                s_next = scores_t(chunks[j + 1][0]())
            m = soft_pv(s, get_vt(), m)
        acc = acc_ref[...]
        o_t = acc[:LANES] * (1.0 / acc[LANES:LANES + 1])
        for c in range(acc.shape[1] // (2 * tq)):
            a = o_t[:HALF, 2 * c * tq:(2 * c + 1) * tq]
            b = o_t[HALF:, (2 * c + 1) * tq:(2 * c + 2) * tq]
            o2 = jnp.concatenate([a, b], axis=0)
            o_ref[:, c * LANES:(c + 1) * LANES] = o2.T.astype(o_ref.dtype)

    ctx_chunk = (lambda: kc_ref[...], lambda: vtc_ref[...])
    lat_chunks = [
        (functools.partial(lambda j: kl_ref[j * tk:(j + 1) * tk, :], j),
         functools.partial(lambda j: vtl_ref[..., j * tk:(j + 1) * tk], j))
        for j in range(n_lat_chunks)]

    @pl.when(qt < nq_lat)
    def _():
        run([ctx_chunk] + lat_chunks)

    if ctx_queries:
        @pl.when(qt >= nq_lat)
        def _():
            run([ctx_chunk])


def _attention(q, k, vt, *, B, S, L, shared_k, ctx_queries):
    T = q.shape[0]
    tq = L
    tk = min(512, S)
    nq_lat = S // tq
    nq = nq_lat + (1 if ctx_queries else 0)
    ctx_blk0 = B * S // L

    def qrow(b, qt):
        return jnp.where(qt < nq_lat, b * nq_lat + qt, ctx_blk0 + b)

    if shared_k:
        units, heads, out_w = A_KV_HEADS, A_HEADS // A_KV_HEADS, 2 * LANES
        q_spec = pl.BlockSpec((tq, 2 * LANES), lambda b, p, qt: (qrow(b, qt), p))
        kl_spec = pl.BlockSpec((None, S, LANES), lambda b, p, qt: (p, b, 0))
        kc_spec = pl.BlockSpec((None, L, LANES), lambda b, p, qt: (p, ctx_blk0 + b, 0))
        vl_spec = pl.BlockSpec((None, VT_ROWS, S), lambda b, p, qt: (p, 0, b))
        vc_spec = pl.BlockSpec((None, VT_ROWS, L), lambda b, p, qt: (p, 0, ctx_blk0 + b))
    else:
        units, heads, out_w = M_HEADS // 4, 4, 2 * LANES
        q_spec = pl.BlockSpec((tq, 4 * LANES), lambda b, p, qt: (qrow(b, qt), p))
        kl_spec = pl.BlockSpec((S, 4 * LANES), lambda b, p, qt: (b, p))
        kc_spec = pl.BlockSpec((L, 4 * LANES), lambda b, p, qt: (ctx_blk0 + b, p))
        vl_spec = pl.BlockSpec((2, VT_ROWS, S), lambda b, p, qt: (p, 0, b))
        vc_spec = pl.BlockSpec((2, VT_ROWS, L), lambda b, p, qt: (p, 0, ctx_blk0 + b))
    return pl.pallas_call(
        functools.partial(_attn_kernel, shared_k=shared_k, tq=tq, tk=tk,
                          n_lat_chunks=S // tk, nq_lat=nq_lat, ctx_queries=ctx_queries),
        grid=(B, units, nq),
        in_specs=[q_spec, kl_spec, kc_spec, vl_spec, vc_spec],
        out_specs=pl.BlockSpec((tq, out_w), lambda b, p, qt: (qrow(b, qt), p)),
        out_shape=jax.ShapeDtypeStruct((T if ctx_queries else B * S, units * out_w), BF16),
        scratch_shapes=[pltpu.VMEM((VT_ROWS, heads * tq), F32)],
        compiler_params=_cparams(3),
        name="attention",
    )(q, k, k, vt, vt)


def _linear_resid_kernel(a_ref, w_ref, x_ref, mod_ref, g_ref, o_ref, *, gate_row):
    y = _dot(a_ref[...], w_ref[...])
    o_ref[...] = x_ref[...] + mod_ref[gate_row:gate_row + 1, :] * _rms(y, g_ref[...])


def _linear_resid(A, W, X, mod, g, geom, *, gate_row, n_tiles):
    tm = geom.tm
    return pl.pallas_call(
        functools.partial(_linear_resid_kernel, gate_row=gate_row),
        grid=(n_tiles,),
        in_specs=[
            _row_spec(tm, A.shape[1]), _full_spec(W.shape), _row_spec(tm, D_MODEL),
            geom.mod_spec(), _full_spec((1, D_MODEL)),
        ],
        out_specs=_row_spec(tm, D_MODEL),
        out_shape=jax.ShapeDtypeStruct((n_tiles * tm, D_MODEL), F32),
        compiler_params=_cparams(1),
        name="linear_resid",
    )(A, W, X, mod, g)


def _norm_linear_kernel(x_ref, mod_ref, g_ref, w_ref, o_ref):
    h = _norm_mod(x_ref[...], g_ref[...], mod_ref[0:1, :], mod_ref[1:2, :]).astype(BF16)
    o_ref[...] = _dot(h, w_ref[...]).astype(o_ref.dtype)


def _norm_linear(X, mod, g, W, geom):
    tm = geom.tm
    n = W.shape[1]
    return pl.pallas_call(
        _norm_linear_kernel,
        grid=(geom.n_all,),
        in_specs=[_row_spec(tm, D_MODEL), geom.mod_spec(), _full_spec((1, D_MODEL)),
                  _full_spec(W.shape)],
        out_specs=_row_spec(tm, n),
        out_shape=jax.ShapeDtypeStruct((X.shape[0], n), BF16),
        compiler_params=_cparams(1),
        name="norm_linear",
    )(X, mod, g, W)


def _seq_dft_kernel(c_ref, s_ref, y1_ref, y2_ref, o_ref, acc_ref):
    k = pl.program_id(2)

    @pl.when(k == 0)
    def _():
        acc_ref[...] = jnp.zeros(acc_ref.shape, F32)

    acc_ref[...] += _dot(c_ref[...], y1_ref[...]) + _dot(s_ref[...], y2_ref[...])

    @pl.when(k == pl.num_programs(2) - 1)
    def _():
        o_ref[...] = acc_ref[...].astype(o_ref.dtype)


def _seq_dft(cmat, smat, Y, *, n_batch, n_seq, row_blk0):
    tu = min(1024, n_seq)
    nu = n_seq // tu
    return pl.pallas_call(
        _seq_dft_kernel,
        grid=(n_batch, nu, nu),
        in_specs=[
            pl.BlockSpec((tu, tu), lambda b, u, k: (u, k)),
            pl.BlockSpec((tu, tu), lambda b, u, k: (u, k)),
            pl.BlockSpec((tu, D_MODEL), lambda b, u, k: (row_blk0 + b * nu + k, 0)),
            pl.BlockSpec((tu, D_MODEL), lambda b, u, k: (row_blk0 + b * nu + k, 1)),
        ],
        out_specs=pl.BlockSpec((tu, D_MODEL), lambda b, u, k: (b * nu + u, 0)),
        out_shape=jax.ShapeDtypeStruct((n_batch * n_seq, D_MODEL), BF16),
        scratch_shapes=[pltpu.VMEM((tu, D_MODEL), F32)],
        compiler_params=_cparams(3),
        name="seq_dft",
    )(cmat, smat, Y, Y)


def _swiglu_chunks(h, wgu_ref, wd_ref, acc_ref, ff, tf, between=None):
    n = ff // tf
    for c in range(n):
        g = _dot(h, wgu_ref[:, c * tf:(c + 1) * tf])
        u = _dot(h, wgu_ref[:, ff + c * tf:ff + (c + 1) * tf])
        a = ((g * jax.nn.sigmoid(g)) * u).astype(BF16)
        part = _dot(a, wd_ref[c * tf:(c + 1) * tf, :])
        if c == 0:
            acc_ref[...] = part
        else:
            acc_ref[...] += part
        if between is not None:
            between(c, n)


def _ffn_dense_kernel(x_ref, mod_ref, g2_ref, g3_ref, wgu_ref, wd_ref, o_ref, acc_ref,
                      *, ff, tf):
    x = x_ref[...]
    h = _norm_mod(x, g2_ref[...], mod_ref[3:4, :], mod_ref[4:5, :]).astype(BF16)
    _swiglu_chunks(h, wgu_ref, wd_ref, acc_ref, ff, tf)
    o_ref[...] = x + mod_ref[5:6, :] * _rms(acc_ref[...], g3_ref[...])


def _ffn_dense(X, mod, g2, g3, wgu, wd, geom):
    tm = geom.tm
    ff = wd.shape[0]
    single = pl.Buffered(1)
    return pl.pallas_call(
        functools.partial(_ffn_dense_kernel, ff=ff, tf=256),
        grid=(geom.n_all,),
        in_specs=[
            _row_spec(tm, D_MODEL), geom.mod_spec(), _full_spec((1, D_MODEL)),
            _full_spec((1, D_MODEL)),
            pl.BlockSpec(wgu.shape, lambda i: (0, 0), pipeline_mode=single),
            pl.BlockSpec(wd.shape, lambda i: (0, 0), pipeline_mode=single),
        ],
        out_specs=_row_spec(tm, D_MODEL),
        out_shape=jax.ShapeDtypeStruct(X.shape, F32),
        scratch_shapes=[pltpu.VMEM((tm, D_MODEL), F32)],
        compiler_params=_cparams(1),
        name="ffn_dense",
    )(X, mod, g2, g3, wgu, wd)


def _router_kernel(x_ref, mod_ref, g_ref, wr_ref, h_ref, r_ref):
    h = _norm_mod(x_ref[...], g_ref[...], mod_ref[3:4, :], mod_ref[4:5, :])
    h_ref[...] = h
    logits = jnp.dot(h, wr_ref[...], precision=lax.Precision.HIGHEST,
                     preferred_element_type=F32)
    lane = lax.broadcasted_iota(jnp.int32, logits.shape, 1)
    lane_f = lane.astype(F32)
    neg = jnp.float32(-jnp.inf)
    logits = jnp.where(lane < N_EXPERTS, logits, neg)
    v1 = jnp.max(logits, axis=-1, keepdims=True)
    i1 = jnp.min(jnp.where(logits == v1, lane_f, float(LANES)), axis=-1, keepdims=True)
    rest = jnp.where(lane_f == i1, neg, logits)
    v2 = jnp.max(rest, axis=-1, keepdims=True)
    i2 = jnp.min(jnp.where(rest == v2, lane_f, float(LANES)), axis=-1, keepdims=True)
    e2 = jnp.exp(v2 - v1)
    den = 1.0 + e2
    w1 = 1.0 / den
    w2 = e2 / den
    out = jnp.where(lane == 0, w1, 0.0)
    out = jnp.where(lane == 1, w2, out)
    out = jnp.where(lane == 2, i1, out)
    out = jnp.where(lane == 3, i2, out)
    r_ref[...] = out


def _router(X, mod, g, wr, geom, n_tiles):
    tm = geom.tm
    rows = n_tiles * tm
    return pl.pallas_call(
        _router_kernel,
        grid=(n_tiles,),
        in_specs=[_row_spec(tm, D_MODEL), geom.mod_spec(), _full_spec((1, D_MODEL)),
                  _full_spec(wr.shape)],
        out_specs=[_row_spec(tm, D_MODEL), _row_spec(tm, LANES)],
        out_shape=[jax.ShapeDtypeStruct((rows, D_MODEL), F32),
                   jax.ShapeDtypeStruct((rows, LANES), F32)],
        compiler_params=_cparams(1),
        name="moe_router",
    )(X, mod, g, wr)


GATHER_UNROLL = 8


def _start_row(src_hbm, idx_ref, dst, sem, r, priority):
    t = idx_ref[0, 0, r]
    pltpu.make_async_copy(src_hbm.at[pl.ds(t, 1), :], dst.at[pl.ds(r, 1), :], sem).start(
        priority=priority)


def _start_row_gather(src_hbm, idx_ref, dst, sem, n_rows):
    def body(g, carry):
        for u in range(GATHER_UNROLL):
            _start_row(src_hbm, idx_ref, dst, sem, g * GATHER_UNROLL + u, u % 2)
        return carry

    lax.fori_loop(0, n_rows // GATHER_UNROLL, body, 0)


def _start_rows_static(src_hbm, idx_ref, dst, sem, r0, r1):
    for r in range(r0, r1):
        _start_row(src_hbm, idx_ref, dst, sem, r, r % 2)


def _wait_row_gather(src_hbm, dst, sem, n_rows):
    pltpu.make_async_copy(src_hbm.at[pl.ds(0, n_rows), :], dst, sem).wait()


def _moe_ffn_kernel(te_ref, nu_ref, idx_ref, idx_next_ref, h_hbm, wgu_ref, wd_ref, o_ref,
                    gbuf, sem, acc_ref, *, tm, ff, tf):
    i = pl.program_id(0)
    n_used = nu_ref[0]
    slot = i % 2

    @pl.when(i == 0)
    def _():
        _start_row_gather(h_hbm, idx_ref, gbuf.at[0], sem.at[0], tm)

    _wait_row_gather(h_hbm, gbuf.at[slot], sem.at[slot], tm)

    @pl.when(i < n_used)
    def _():
        h = gbuf[slot].astype(BF16)

        def issue_next(c, n):
            _start_rows_static(h_hbm, idx_next_ref, gbuf.at[1 - slot], sem.at[1 - slot],
                               c * tm // n, (c + 1) * tm // n)

        _swiglu_chunks(h, wgu_ref, wd_ref, acc_ref, ff, tf, issue_next)
        o_ref[...] = acc_ref[...]

    @pl.when(i >= n_used)
    def _():
        _start_row_gather(h_hbm, idx_next_ref, gbuf.at[1 - slot], sem.at[1 - slot], tm)
        o_ref[...] = jnp.zeros(o_ref.shape, o_ref.dtype)

    @pl.when(i == pl.num_programs(0) - 1)
    def _():
        _wait_row_gather(h_hbm, gbuf.at[1 - slot], sem.at[1 - slot], tm)


def _moe_ffn(h2, wgu, wd, tile_expert, n_used, slot_token, *, tm, n_tiles):
    ff = wd.shape[1]
    single = pl.Buffered(1)
    grid_spec = pltpu.PrefetchScalarGridSpec(
        num_scalar_prefetch=2,
        grid=(n_tiles,),
        in_specs=[
            pl.BlockSpec((1, 1, tm), lambda i, te, nu: (i, 0, 0), memory_space=pltpu.SMEM),
            pl.BlockSpec((1, 1, tm), lambda i, te, nu: (jnp.minimum(i + 1, n_tiles - 1), 0, 0),
                         memory_space=pltpu.SMEM),
            pl.BlockSpec(memory_space=pl.ANY),
            pl.BlockSpec((None, D_MODEL, 2 * ff), lambda i, te, nu: (te[i], 0, 0),
                         pipeline_mode=single),
            pl.BlockSpec((None, ff, D_MODEL), lambda i, te, nu: (te[i], 0, 0),
                         pipeline_mode=single),
        ],
        out_specs=pl.BlockSpec((tm, D_MODEL), lambda i, te, nu: (i, 0)),
        scratch_shapes=[
            pltpu.VMEM((2, tm, D_MODEL), F32),
            pltpu.SemaphoreType.DMA((2,)),
            pltpu.VMEM((tm, D_MODEL), F32),
        ],
    )
    return pl.pallas_call(
        functools.partial(_moe_ffn_kernel, tm=tm, ff=ff, tf=512),
        grid_spec=grid_spec,
        out_shape=jax.ShapeDtypeStruct((n_tiles * tm, D_MODEL), F32),
        compiler_params=_cparams(1),
        name="moe_ffn",
    )(tile_expert, n_used, slot_token, slot_token, h2, wgu, wd)


def _moe_combine_kernel(idx_ref, idx_next_ref, y_hbm, r_ref, x_ref, mod_ref, g_ref, o_ref,
                        gbuf, sem, *, tm):
    i = pl.program_id(0)
    n = pl.num_programs(0)
    slot = i % 2

    @pl.when(i == 0)
    def _():
        _start_row_gather(y_hbm, idx_ref, gbuf.at[0], sem.at[0], 2 * tm)

    _wait_row_gather(y_hbm, gbuf.at[slot], sem.at[slot], 2 * tm)

    @pl.when(i + 1 < n)
    def _():
        _start_row_gather(y_hbm, idx_next_ref, gbuf.at[1 - slot], sem.at[1 - slot], 2 * tm)

    r = r_ref[...]
    y = r[:, 0:1] * gbuf[slot, 0:tm, :] + r[:, 1:2] * gbuf[slot, tm:2 * tm, :]
    o_ref[...] = x_ref[...] + mod_ref[5:6, :] * _rms(y, g_ref[...])


def _moe_combine(Y, tok_slot, route, X, mod, g, geom, n_tiles):
    tm = geom.tm
    return pl.pallas_call(
        functools.partial(_moe_combine_kernel, tm=tm),
        grid=(n_tiles,),
        in_specs=[
            pl.BlockSpec((1, 1, 2 * tm), lambda i: (i, 0, 0), memory_space=pltpu.SMEM),
            pl.BlockSpec((1, 1, 2 * tm), lambda i: (jnp.minimum(i + 1, n_tiles - 1), 0, 0),
                         memory_space=pltpu.SMEM),
            pl.BlockSpec(memory_space=pl.ANY),
            _row_spec(tm, LANES), _row_spec(tm, D_MODEL), geom.mod_spec(),
            _full_spec((1, D_MODEL)),
        ],
        out_specs=_row_spec(tm, D_MODEL),
        out_shape=jax.ShapeDtypeStruct((n_tiles * tm, D_MODEL), F32),
        scratch_shapes=[pltpu.VMEM((2, 2 * tm, D_MODEL), F32), pltpu.SemaphoreType.DMA((2,))],
        compiler_params=_cparams(1),
        name="moe_combine",
    )(tok_slot, tok_slot, Y, route, X, mod, g)


def _moe_layer(X, mod, g2, g3, w_router, wgu, wd, geom, n_tiles, tm_e):
    tm = geom.tm
    rows = n_tiles * tm
    wr = jnp.pad(w_router, ((0, 0), (0, LANES - N_EXPERTS)))
    h2, route = _router(X, mod, g2, wr, geom, n_tiles)

    e_flat = route[:, 2:4].astype(jnp.int32).reshape(-1)
    onehot = (e_flat[:, None] == jnp.arange(N_EXPERTS, dtype=jnp.int32)[None, :]).astype(jnp.int32)
    csum = jnp.cumsum(onehot, axis=0)
    rank = jnp.sum(csum * onehot, axis=1) - 1
    counts = csum[-1]
    tiles_e = (counts + tm_e - 1) // tm_e
    tile_end = jnp.cumsum(tiles_e)
    tile_start = tile_end - tiles_e
    n_used = tile_end[-1]
    slot = tile_start[e_flat] * tm_e + rank
    n_tiles_e = (2 * rows) // tm_e + N_EXPERTS
    tok = jnp.arange(2 * rows, dtype=jnp.int32) // 2
    slot_token = jnp.zeros((n_tiles_e * tm_e,), jnp.int32).at[slot].set(tok)
    tile_ids = jnp.minimum(jnp.arange(n_tiles_e, dtype=jnp.int32), n_used - 1)
    tile_expert = jnp.sum((tile_ids[:, None] >= tile_end[None, :]).astype(jnp.int32), axis=1)
    tile_expert = jnp.minimum(tile_expert, N_EXPERTS - 1).astype(jnp.int32)

    Y = _moe_ffn(h2, wgu, wd, tile_expert, n_used.reshape(1).astype(jnp.int32),
                 slot_token.reshape(n_tiles_e, 1, tm_e), tm=tm_e, n_tiles=n_tiles_e)
    slot2 = slot.reshape(rows, 2)
    tok_slot = jnp.concatenate(
        [slot2[:, 0].reshape(n_tiles, 1, tm), slot2[:, 1].reshape(n_tiles, 1, tm)], axis=2)
    return _moe_combine(Y, tok_slot.astype(jnp.int32), route, X, mod, g3, geom, n_tiles)


def _rope_tables(S, tm, rot_dim, lane0, reps):
    nf = rot_dim // 4
    t = jnp.arange(S, dtype=jnp.int32)
    rows = (t // GRID_W).astype(F32)
    cols = (t % GRID_W).astype(F32)
    inv = ROPE_BASE ** (-jnp.arange(nf, dtype=F32) / nf)
    ang_r = rows[:, None] * inv[None, :]
    ang_c = cols[:, None] * inv[None, :]
    cr, sr, cc, sc = jnp.cos(ang_r), jnp.sin(ang_r), jnp.cos(ang_c), jnp.sin(ang_c)
    cos_blk = jnp.concatenate([cr, cr, cc, cc], axis=1)
    sin_blk = jnp.concatenate([-sr, sr, -sc, sc], axis=1)
    cos = jnp.ones((S, LANES), F32)
    sin = jnp.zeros((S, LANES), F32)
    for r in range(reps):
        a = lane0 + r * rot_dim
        cos = cos.at[:, a:a + rot_dim].set(cos_blk)
        sin = sin.at[:, a:a + rot_dim].set(sin_blk)
    cos = jnp.concatenate([cos, jnp.ones((tm, LANES), F32)], axis=0)
    sin = jnp.concatenate([sin, jnp.zeros((tm, LANES), F32)], axis=0)
    return cos, sin


def _dft_mats(n, scale):
    u = jnp.arange(n, dtype=jnp.int32)
    idx = (u[:, None] * u[None, :]) % n
    ang = idx.astype(F32) * (2.0 * math.pi / n)
    return (jnp.cos(ang) * scale).astype(BF16), (-jnp.sin(ang) * scale).astype(BF16)


def _channel_dft_weight():
    n = D_MODEL // F_GROUPS
    v = jnp.arange(n, dtype=jnp.int32)
    ang = ((v[:, None] * v[None, :]) % n).astype(F32) * (2.0 * math.pi / n)
    eye = jnp.eye(F_GROUPS, dtype=F32)
    wc = jnp.kron(eye, jnp.cos(ang))
    ws = jnp.kron(eye, jnp.sin(ang))
    return jnp.concatenate([wc, ws], axis=1).astype(BF16)


def kernel(x, c, ctx, c_ctx, w_mod, b_mod, norm_g, a_w_qkv, a_g_q, a_g_k, a_w_o, f_w,
           m_w_down, m_g_cq, m_g_ckv, m_w_uq, m_w_ukv, m_w_o, d_w_gu, d_w_down,
           e_w_router, e_w_gu, e_w_down):
    B, S, D = x.shape
    L = ctx.shape[1]
    depth = w_mod.shape[0]
    assert D == D_MODEL and B + 1 <= N_MOD_ROWS and S % L == 0
    tm = min(512, S)
    geom = _Geom(B, S, L, tm)
    n_lat, n_all = geom.n_lat, geom.n_all

    X = jnp.concatenate([x.reshape(B * S, D), ctx.reshape(B * L, D)], axis=0)
    cvec = jnp.zeros((N_MOD_ROWS, D), F32).at[:B].set(c).at[B].set(c_ctx)
    mods = _mod_vectors(cvec, w_mod, b_mod)

    rope_a = _rope_tables(S, tm, A_HEAD_DIM, 0, 2)
    rope_m = _rope_tables(S, tm, M_ROPE, M_NOPE, 1)

    for i in range(depth):
        last = i == depth - 1
        kind = i % N_MIXERS
        j = i // N_MIXERS
        mod = mods[i]
        g = norm_g[i].reshape(4, 1, D)
        n_out = n_lat if last else n_all

        if kind == 0:
            gq = jnp.tile(a_g_q[j], 2).reshape(1, LANES)
            gk = jnp.tile(a_g_k[j], 2).reshape(1, LANES)
            q, kd, vd = _gqa_proj(X, mod, g[0], a_w_qkv[j].astype(BF16), gq, gk,
                                  rope_a[0], rope_a[1], geom)
            att = _attention(q, kd, vd, B=B, S=S, L=L, shared_k=True, ctx_queries=not last)
            X = _linear_resid(att, a_w_o[j].astype(BF16), X, mod, g[1], geom,
                              gate_row=2, n_tiles=n_out)
        elif kind == 1:
            Y = _norm_linear(X, mod, g[0], _channel_dft_weight(), geom)
            n_ch = D // F_GROUPS
            cl, sl = _dft_mats(S, 1.0 / math.sqrt(S * n_ch))
            Z = _seq_dft(cl, sl, Y, n_batch=B, n_seq=S, row_blk0=0)
            if not last:
                cc_, sc_ = _dft_mats(L, 1.0 / math.sqrt(L * n_ch))
                Zc = _seq_dft(cc_, sc_, Y, n_batch=B, n_seq=L, row_blk0=B * S // L)
                Z = jnp.concatenate([Z, Zc], axis=0)
            X = _linear_resid(Z, f_w[j].astype(BF16), X, mod, g[1], geom,
                              gate_row=2, n_tiles=n_out)
        else:
            wd = m_w_down[j]
            wdn = jnp.concatenate([
                wd[:, :M_Q_LORA + M_KV_LORA],
                jnp.zeros((D, M_NOPE), F32), wd[:, M_Q_LORA + M_KV_LORA:],
                jnp.zeros((D, LANES - M_NOPE - M_ROPE), F32)], axis=1).astype(BF16)
            wuq = jnp.pad(m_w_uq[j].reshape(M_Q_LORA, M_HEADS, M_NOPE + M_ROPE),
                          ((0, 0), (0, 0), (0, LANES - M_NOPE - M_ROPE)))
            wuq = wuq.reshape(M_Q_LORA, M_HEADS * LANES).astype(BF16)
            wukv = m_w_ukv[j].reshape(M_KV_LORA, M_HEADS, M_NOPE + M_V)
            wuk = jnp.pad(wukv[:, :, :M_NOPE], ((0, 0), (0, 0), (0, LANES - M_NOPE)))
            wuk = wuk.reshape(M_KV_LORA, M_HEADS * LANES).astype(BF16)
            wuv = wukv[:, :, M_NOPE:].reshape(M_KV_LORA, M_HEADS * M_V).astype(BF16)
            q, k, v = _mla_proj(X, mod, g[0], wdn, m_g_cq[j].reshape(1, -1),
                                m_g_ckv[j].reshape(1, -1), wuq, wuk, wuv,
                                rope_m[0], rope_m[1], geom)
            att = _attention(q, k, v, B=B, S=S, L=L, shared_k=False, ctx_queries=not last)
            X = _linear_resid(att, m_w_o[j].astype(BF16), X, mod, g[1], geom,
                              gate_row=2, n_tiles=n_out)

        if i % 2 == 0:
            assert not last
            X = _ffn_dense(X, mod, g[2], g[3], d_w_gu[i // 2].astype(BF16),
                           d_w_down[i // 2].astype(BF16), geom)
        else:
            X = _moe_layer(X, mod, g[2], g[3], e_w_router[i // 2],
                           e_w_gu[i // 2].astype(BF16), e_w_down[i // 2].astype(BF16),
                           geom, n_out, min(512, S))

    return X[:B * S].reshape(B, S, D)
```
